```python
import functools
import jax, jax.numpy as jnp
from jax import lax
import numpy as np

D_MODEL = 2048
BATCH = 8
SEQ = 2048
DEPTH = 2
DEC_BATCH = 32
DEC_SEQ = 1
PAST_LEN = 8192
PAGE_SIZE = 128

HEAD_DIM = 128
D_ATT = D_MODEL // 2
N_HEADS = D_ATT // HEAD_DIM
GROUP_DIM = 128
D_SGU = D_MODEL // 2
N_GROUPS = D_SGU // GROUP_DIM
MOBA_BLOCK = 256
TOPK = 3
QBLK = 128
CHUNK = 128
EPS = 1e-6
SCALE = HEAD_DIM ** -0.5
SPLITS = [D_ATT, 2 * D_ATT, 3 * D_ATT, 4 * D_ATT, 4 * D_ATT + D_SGU, 4 * D_ATT + 2 * D_SGU, 4 * D_ATT + 3 * D_SGU, 4 * D_ATT + 3 * D_SGU + D_MODEL]
N_IN = 4 * D_ATT + 3 * D_SGU + 2 * D_MODEL

kernel_name = 'moba_sgu_hybrid_step'


def _rmsnorm(x, g):
    xf = x.astype(jnp.float32)
    y = xf * lax.rsqrt(jnp.mean(xf * xf, axis=-1, keepdims=True) + EPS)
    return (y * g.astype(jnp.float32)).astype(x.dtype)


def _layernorm(x, g, b):
    xf = x.astype(jnp.float32)
    mu = jnp.mean(xf, axis=-1, keepdims=True)
    var = jnp.mean(jnp.square(xf - mu), axis=-1, keepdims=True)
    return ((xf - mu) * lax.rsqrt(var + EPS) * g.astype(jnp.float32) + b.astype(jnp.float32)).astype(x.dtype)


def _alibi_slopes():
    return 2.0 ** (-8.0 * jnp.arange(1, N_HEADS + 1, dtype=jnp.float32) / N_HEADS)


def moba_prompt(q, k, v):
    b, s, h, d = q.shape
    nbc = s // MOBA_BLOCK
    nba = -(-s // MOBA_BLOCK)
    nqb = s // QBLK
    pad = ((0, 0), (0, nba * MOBA_BLOCK - s), (0, 0), (0, 0))
    kh = jnp.pad(k, pad).reshape(b, nba, MOBA_BLOCK, h, d).transpose(0, 3, 1, 2, 4)
    vh = jnp.pad(v, pad).reshape(b, nba, MOBA_BLOCK, h, d).transpose(0, 3, 1, 2, 4)
    slopes = _alibi_slopes()
    n_sel = min(TOPK, nbc)
    qs = q.reshape(b, nqb, QBLK, h, d).transpose(0, 1, 3, 2, 4).reshape(b * nqb, h, QBLK, d)
    xs = (qs, jnp.repeat(jnp.arange(b, dtype=jnp.int32), nqb), jnp.tile(jnp.arange(nqb, dtype=jnp.int32), b))
    if n_sel > 0:
        kmean = jnp.mean(kh[:, :, :nbc].astype(jnp.float32), axis=3)
        gate = jnp.einsum('bshd,bhnd->bhsn', q.astype(jnp.float32), kmean)
        eligible = jnp.arange(nbc)[None, :] < (jnp.arange(s) // MOBA_BLOCK)[:, None]
        gval, sel = lax.top_k(jnp.where(eligible, gate, -jnp.inf), n_sel)
        regroup = lambda a: a.reshape(b, h, nqb, QBLK, n_sel).transpose(0, 2, 1, 3, 4).reshape(b * nqb, h, QBLK, n_sel)
        xs = xs + (regroup(sel), regroup(jnp.isfinite(gval)))
    hidx = jnp.arange(h)[:, None, None]

    def step(args):
        qb, bi, iq = args[0], args[1], args[2]
        pos_q = iq * QBLK + jnp.arange(QBLK)
        jb = (iq * QBLK) // MOBA_BLOCK
        k_own, v_own = kh[bi, :, jb], vh[bi, :, jb]
        pos_own = jb * MOBA_BLOCK + jnp.arange(MOBA_BLOCK)
        dist = (pos_q[:, None] - pos_own[None, :]).astype(jnp.float32)
        sc = jnp.einsum('hqd,hkd->hqk', qb, k_own).astype(jnp.float32) * SCALE - slopes[:, None, None] * dist
        sc = jnp.where(dist >= 0, sc, -jnp.inf)
        if n_sel == 0:
            p = jax.nn.softmax(sc, axis=-1).astype(v.dtype)
            return jnp.einsum('hqk,hkd->hqd', p, v_own)
        sel_b, val_b = args[3], args[4]
        k_sel, v_sel = kh[bi, hidx, sel_b], vh[bi, hidx, sel_b]
        pos_sel = sel_b[..., None] * MOBA_BLOCK + jnp.arange(MOBA_BLOCK)
        dist_sel = (pos_q[None, :, None, None] - pos_sel).astype(jnp.float32)
        ss = jnp.einsum('hqd,hqrkd->hqrk', qb, k_sel).astype(jnp.float32) * SCALE - slopes[:, None, None, None] * dist_sel
        ss = jnp.where(val_b[..., None], ss, -jnp.inf)
        p = jax.nn.softmax(jnp.concatenate([sc, ss.reshape(h, QBLK, n_sel * MOBA_BLOCK)], axis=-1), axis=-1).astype(v.dtype)
        return (jnp.einsum('hqk,hkd->hqd', p[..., :MOBA_BLOCK], v_own)
                + jnp.einsum('hqrk,hqrkd->hqd', p[..., MOBA_BLOCK:].reshape(h, QBLK, n_sel, MOBA_BLOCK), v_sel))

    o = lax.map(step, xs)
    return o.reshape(b, nqb, h, QBLK, d).transpose(0, 1, 3, 2, 4).reshape(b, s, h * d)


def moba_sample(q, k_new, v_new, k_pool, v_pool, page_table, layer):
    b, t, h, d = q.shape
    page = k_pool.shape[2]
    n_pages = page_table.shape[1]
    past = n_pages * page
    slopes = _alibi_slopes()
    pos_q = past + jnp.arange(t)
    blk_q = pos_q // MOBA_BLOCK
    nlp = min(MOBA_BLOCK, past) // page
    loc = page_table[:, n_pages - nlp:]
    k_own = jnp.concatenate([k_pool[layer, loc].reshape(b, nlp * page, h, d), k_new], axis=1)
    v_own = jnp.concatenate([v_pool[layer, loc].reshape(b, nlp * page, h, d), v_new], axis=1)
    pos_own = jnp.concatenate([past - nlp * page + jnp.arange(nlp * page), pos_q])
    dist = (pos_q[:, None] - pos_own[None, :]).astype(jnp.float32)
    sc = jnp.einsum('bqhd,bkhd->bhqk', q, k_own).astype(jnp.float32) * SCALE - slopes[:, None, None] * dist
    own = (pos_own[None, :] >= (blk_q * MOBA_BLOCK)[:, None]) & (dist >= 0)
    sc = jnp.where(own, sc, -jnp.inf)
    nbp = past // MOBA_BLOCK
    n_sel = min(TOPK, nbp)
    if n_sel == 0:
        p = jax.nn.softmax(sc, axis=-1).astype(v_new.dtype)
        o = jnp.einsum('bhqk,bkhd->bqhd', p, v_own)
    else:
        ppb = MOBA_BLOCK // page
        blk_pages = page_table[:, :nbp * ppb].reshape(b, nbp, ppb)
        kmean = jnp.mean(k_pool[layer, blk_pages].astype(jnp.float32), axis=(2, 3))
        gate = jnp.einsum('bqhd,bnhd->bhqn', q.astype(jnp.float32), kmean)
        eligible = jnp.arange(nbp)[None, :] < blk_q[:, None]
        gval, sel = lax.top_k(jnp.where(eligible, gate, -jnp.inf), n_sel)
        phys = blk_pages[jnp.arange(b)[:, None, None, None], sel]
        hidx = jnp.arange(h)[None, :, None, None, None, None]
        k_sel = k_pool[layer, phys[..., None], jnp.arange(page), hidx].reshape(b, h, t, n_sel, MOBA_BLOCK, d)
        v_sel = v_pool[layer, phys[..., None], jnp.arange(page), hidx].reshape(b, h, t, n_sel, MOBA_BLOCK, d)
        pos_sel = sel[..., None] * MOBA_BLOCK + jnp.arange(MOBA_BLOCK)
        dist_sel = (pos_q[None, None, :, None, None] - pos_sel).astype(jnp.float32)
        ss = jnp.einsum('bqhd,bhqrkd->bhqrk', q, k_sel).astype(jnp.float32) * SCALE - slopes[None, :, None, None, None] * dist_sel
        ss = jnp.where(jnp.isfinite(gval)[..., None], ss, -jnp.inf)
        p = jax.nn.softmax(jnp.concatenate([sc, ss.reshape(b, h, t, n_sel * MOBA_BLOCK)], axis=-1), axis=-1).astype(v_new.dtype)
        ko = k_own.shape[1]
        o = (jnp.einsum('bhqk,bkhd->bqhd', p[..., :ko], v_own)
             + jnp.einsum('bhqrk,bhqrkd->bqhd', p[..., ko:].reshape(b, h, t, n_sel, MOBA_BLOCK), v_sel))
    return o.reshape(b, t, h * d)


def sgu_prompt(u, vn, ws, bs):
    b, s, _ = u.shape
    vc = vn.reshape(b, s // CHUNK, CHUNK, N_GROUPS, GROUP_DIM)
    mix = jnp.einsum('gts,bnsgc->bntgc', ws, vc) + bs.T[:, :, None]
    return u * mix.reshape(b, s, D_SGU)


def sgu_sample(u, vn, ws, bs):
    b, t, _ = u.shape
    vc = vn.reshape(b, t, N_GROUPS, GROUP_DIM)
    mix = jnp.einsum('gts,bsgc->btgc', ws[:, :t, :t], vc) + bs.T[:t, :, None]
    return u * mix.reshape(b, t, D_SGU)


def _layer(x, c, norm_g, w_ada, b_ada, w_in, ln_g, ln_b, w_pa, w_pb, w_out, attend, spatial):
    mod = jax.nn.silu(c) @ w_ada + b_ada
    shift, scale, gate = jnp.split(mod, 3, axis=-1)
    h = _rmsnorm(x, norm_g) * (1 + scale[:, None]) + shift[:, None]
    q, k, v, za, u, vb, zb, ga, gb = jnp.split(h @ w_in, SPLITS, axis=-1)
    heads = lambda a: a.reshape(a.shape[0], a.shape[1], N_HEADS, HEAD_DIM)
    k4, v4 = heads(k), heads(v)
    att = attend(heads(q), k4, v4)
    vn = _layernorm(jax.nn.gelu(vb, approximate=False), ln_g, ln_b)
    sgu = spatial(jax.nn.gelu(u, approximate=False), vn)
    ya = (att * jax.nn.silu(za)) @ w_pa
    yb = (sgu * jax.nn.silu(zb)) @ w_pb
    m = jax.nn.sigmoid(ga) * ya + jax.nn.sigmoid(gb) * yb
    return x + gate[:, None] * (m @ w_out), k4, v4, vn


def setup_inputs(seed: int = 0) -> dict:
    key = jax.random.key(seed)
    ks = jax.random.split(key, 22)
    n_pages = PAST_LEN // PAGE_SIZE
    n_pool = (5 * DEC_BATCH * n_pages + 3) // 4
    nrm = lambda k, shape, s=1.0: s * jax.random.normal(k, shape, jnp.float32)
    page_table = jax.random.permutation(ks[4], n_pool)[:DEC_BATCH * n_pages].reshape(DEC_BATCH, n_pages).astype(jnp.int32)
    return {
        'x_prompt': nrm(ks[0], (BATCH, SEQ, D_MODEL)),
        'x_sample': nrm(ks[1], (DEC_BATCH, DEC_SEQ, D_MODEL)),
        'cache_k': nrm(ks[2], (DEPTH, n_pool, PAGE_SIZE, N_HEADS, HEAD_DIM)),
        'cache_v': nrm(ks[3], (DEPTH, n_pool, PAGE_SIZE, N_HEADS, HEAD_DIM)),
        'page_table': page_table,
        'c_prompt': nrm(ks[5], (BATCH, D_MODEL)),
        'c_sample': nrm(ks[6], (DEC_BATCH, D_MODEL)),
        'norm_g': 1.0 + nrm(ks[7], (DEPTH, D_MODEL), 0.1),
        'w_ada': nrm(ks[8], (DEPTH, D_MODEL, 3 * D_MODEL), 0.5 * D_MODEL ** -0.5),
        'b_ada': nrm(ks[9], (DEPTH, 3 * D_MODEL), 0.01),
        'w_in': nrm(ks[10], (DEPTH, D_MODEL, N_IN), D_MODEL ** -0.5),
        'sgu_ln_g': 1.0 + nrm(ks[11], (DEPTH, D_SGU), 0.1),
        'sgu_ln_b': nrm(ks[12], (DEPTH, D_SGU), 0.01),
        'w_s': nrm(ks[13], (DEPTH, N_GROUPS, CHUNK, CHUNK), CHUNK ** -0.5),
        'b_s': nrm(ks[14], (DEPTH, N_GROUPS, CHUNK), 0.1),
        'w_proj_a': nrm(ks[15], (DEPTH, D_ATT, D_MODEL), D_ATT ** -0.5),
        'w_proj_b': nrm(ks[16], (DEPTH, D_SGU, D_MODEL), D_SGU ** -0.5),
        'w_out': nrm(ks[17], (DEPTH, D_MODEL, D_MODEL), D_MODEL ** -0.5),
        'final_g': 1.0 + nrm(ks[18], (D_MODEL,), 0.1),
    }


def reference(x_prompt, x_sample, cache_k, cache_v, page_table, c_prompt, c_sample, norm_g, w_ada, b_ada, w_in, sgu_ln_g, sgu_ln_b, w_s, b_s, w_proj_a, w_proj_b, w_out, final_g):
    tril = jnp.tril(jnp.ones((CHUNK, CHUNK), dtype=w_s.dtype))
    xp, xs = x_prompt, x_sample
    kp_l, vp_l, ks_l, vs_l, us_l = [], [], [], [], []
    for l in range(DEPTH):
        ws = w_s[l] * tril
        common = (norm_g[l], w_ada[l], b_ada[l], w_in[l], sgu_ln_g[l], sgu_ln_b[l], w_proj_a[l], w_proj_b[l], w_out[l])
        xp, kp, vp, _ = _layer(xp, c_prompt, *common, moba_prompt,
                               functools.partial(sgu_prompt, ws=ws, bs=b_s[l]))
        xs, ksm, vsm, us = _layer(xs, c_sample, *common,
                                  functools.partial(moba_sample, k_pool=cache_k, v_pool=cache_v, page_table=page_table, layer=l),
                                  functools.partial(sgu_sample, ws=ws, bs=b_s[l]))
        kp_l.append(kp)
        vp_l.append(vp)
        ks_l.append(ksm)
        vs_l.append(vsm)
        us_l.append(us)
    y_prompt = _rmsnorm(xp, final_g)
    y_sample = _rmsnorm(xs, final_g)
    return (y_prompt, y_sample, jnp.stack(kp_l), jnp.stack(vp_l), jnp.stack(ks_l), jnp.stack(vs_l), jnp.stack(us_l))
```

```python
import functools

import jax
import jax.numpy as jnp
from jax import lax
from jax.experimental import pallas as pl
from jax.experimental.pallas import tpu as pltpu

D_MODEL = 2048
HEAD_DIM = 128
D_ATT = D_MODEL // 2
N_HEADS = D_ATT // HEAD_DIM
GROUP_DIM = 128
D_SGU = D_MODEL // 2
N_GROUPS = D_SGU // GROUP_DIM
MOBA_BLOCK = 256
TOPK = 3
CHUNK = 128
EPS = 1e-6
SCALE = HEAD_DIM ** -0.5
N_IN = 4 * D_ATT + 3 * D_SGU + 2 * D_MODEL

COL = 1024
N_COLS = N_IN // COL
C_Q, C_K, C_V, C_ZA, C_U, C_VB, C_ZB, C_GA, C_GB = 0, 1, 2, 3, 4, 5, 6, 7, 9

VMEM_LIMIT = 56 * 1024 * 1024
F32 = jnp.float32
BF16 = jnp.bfloat16
NEG_INF = float("-inf")


def _params(*sem):
    return pltpu.CompilerParams(dimension_semantics=sem, vmem_limit_bytes=VMEM_LIMIT)


def _dot(a, b):
    precision = lax.Precision.HIGHEST if a.dtype == F32 else None
    return jnp.dot(a, b, preferred_element_type=F32, precision=precision)


def _gelu(x):
    return 0.5 * x * (1.0 + lax.erf(x * (2.0 ** -0.5)))


def _dot_nt(a, b):
    return lax.dot_general(a, b, (((1,), (1,)), ((), ())), preferred_element_type=F32)


def _mod_kernel(c_ref, w_ref, b_ref, o_ref):
    c = c_ref[...]
    o_ref[...] = _dot(c * jax.nn.sigmoid(c), w_ref[...]) + b_ref[...]


def _mod(c_all, w_ada, b_ada):
    depth, d, n3 = w_ada.shape
    rows = c_all.shape[0]
    tn = 768
    return pl.pallas_call(
        _mod_kernel,
        grid=(depth, n3 // tn),
        in_specs=[
            pl.BlockSpec((rows, d), lambda l, n: (0, 0)),
            pl.BlockSpec((None, d, tn), lambda l, n: (l, 0, n)),
            pl.BlockSpec((None, 1, tn), lambda l, n: (l, 0, n)),
        ],
        out_specs=pl.BlockSpec((None, rows, tn), lambda l, n: (l, 0, n)),
        out_shape=jax.ShapeDtypeStruct((depth, rows, n3), F32),
        compiler_params=_params("arbitrary", "arbitrary"),
        name="adaln_mod",
    )(c_all, w_ada, b_ada.reshape(depth, 1, n3))


def _norm_kernel(x_ref, g_ref, scale_ref, shift_ref, o_ref):
    x = x_ref[...]
    y = x * lax.rsqrt(jnp.mean(x * x, axis=-1, keepdims=True) + EPS) * g_ref[...]
    o_ref[...] = (y * (1.0 + scale_ref[...]) + shift_ref[...]).astype(o_ref.dtype)


def _final_norm_kernel(x_ref, g_ref, o_ref):
    x = x_ref[...]
    o_ref[...] = x * lax.rsqrt(jnp.mean(x * x, axis=-1, keepdims=True) + EPS) * g_ref[...]


def _row_vec_spec(vec, tm, rows_per_batch):
    d = vec.shape[-1]
    if rows_per_batch == 1:
        return vec, pl.BlockSpec((tm, d), lambda m: (m, 0))
    assert rows_per_batch % tm == 0
    per = rows_per_batch // tm
    return vec.reshape(vec.shape[0], 1, d), pl.BlockSpec((None, 1, d), lambda m: (m // per, 0, 0))


def _norm_mod(x, g, scale, shift, tm, rows_per_batch, out_dtype):
    rows, d = x.shape
    scale, scale_spec = _row_vec_spec(scale, tm, rows_per_batch)
    shift, shift_spec = _row_vec_spec(shift, tm, rows_per_batch)
    return pl.pallas_call(
        _norm_kernel,
        grid=(rows // tm,),
        in_specs=[
            pl.BlockSpec((tm, d), lambda m: (m, 0)),
            pl.BlockSpec((1, d), lambda m: (0, 0)),
            scale_spec,
            shift_spec,
        ],
        out_specs=pl.BlockSpec((tm, d), lambda m: (m, 0)),
        out_shape=jax.ShapeDtypeStruct((rows, d), out_dtype),
        compiler_params=_params("arbitrary"),
        name="norm_mod",
    )(x, g.reshape(1, d), scale, shift)


def _final_norm(x, g, tm):
    rows, d = x.shape
    return pl.pallas_call(
        _final_norm_kernel,
        grid=(rows // tm,),
        in_specs=[pl.BlockSpec((tm, d), lambda m: (m, 0)), pl.BlockSpec((1, d), lambda m: (0, 0))],
        out_specs=pl.BlockSpec((tm, d), lambda m: (m, 0)),
        out_shape=jax.ShapeDtypeStruct((rows, d), F32),
        compiler_params=_params("arbitrary"),
        name="final_norm",
    )(x, g.reshape(1, d))


def _inproj_kernel(h_ref, w_ref, lng_ref, lnb_ref, act_ref, k_ref, v_ref, vn_ref, acc_ref):
    n = pl.program_id(1)
    acc_ref[...] = _dot(h_ref[...], w_ref[...])

    @pl.when(n >= C_GA)
    def _():
        act_ref[...] = jax.nn.sigmoid(acc_ref[...]).astype(act_ref.dtype)

    @pl.when(n == C_Q)
    def _():
        act_ref[...] = acc_ref[...].astype(act_ref.dtype)

    @pl.when(n == C_K)
    def _():
        a = acc_ref[...]
        k_ref[...] = a
        act_ref[...] = a.astype(act_ref.dtype)

    @pl.when(n == C_V)
    def _():
        a = acc_ref[...]
        v_ref[...] = a
        act_ref[...] = a.astype(act_ref.dtype)

    @pl.when((n == C_ZA) | (n == C_ZB))
    def _():
        act_ref[...] = jax.nn.silu(acc_ref[...]).astype(act_ref.dtype)

    @pl.when(n == C_U)
    def _():
        act_ref[...] = _gelu(acc_ref[...]).astype(act_ref.dtype)

    @pl.when(n == C_VB)
    def _():
        a = _gelu(acc_ref[...])
        mu = jnp.mean(a, axis=-1, keepdims=True)
        c = a - mu
        var = jnp.mean(c * c, axis=-1, keepdims=True)
        vn = c * lax.rsqrt(var + EPS) * lng_ref[...] + lnb_ref[...]
        vn_ref[...] = vn
        act_ref[...] = vn.astype(act_ref.dtype)


def _inproj(h, w, ln_g, ln_b, tm):
    rows, d = h.shape
    assert h.dtype == w.dtype
    side = jax.ShapeDtypeStruct((rows, COL), F32)
    side_spec = pl.BlockSpec((tm, COL), lambda m, n: (m, 0))
    return pl.pallas_call(
        _inproj_kernel,
        grid=(rows // tm, N_COLS),
        in_specs=[
            pl.BlockSpec((tm, d), lambda m, n: (m, 0)),
            pl.BlockSpec((d, COL), lambda m, n: (0, n)),
            pl.BlockSpec((1, COL), lambda m, n: (0, 0)),
            pl.BlockSpec((1, COL), lambda m, n: (0, 0)),
        ],
        out_specs=[pl.BlockSpec((tm, COL), lambda m, n: (m, n)), side_spec, side_spec, side_spec],
        out_shape=[jax.ShapeDtypeStruct((rows, N_IN), h.dtype), side, side, side],
        scratch_shapes=[pltpu.VMEM((tm, COL), F32)],
        compiler_params=_params("arbitrary", "arbitrary"),
        name="in_proj",
    )(h, w, ln_g.reshape(1, COL), ln_b.reshape(1, COL))


def _moba_prompt_kernel(slopes_ref, q_ref, k_ref, v_ref, za_ref, o_ref, kmean_ref, s_ref, m_ref, l_ref, acc_ref):
    h = pl.program_id(1)
    jb = pl.program_id(2)
    blk = MOBA_BLOCK
    nblk = k_ref.shape[0] // blk
    slope = slopes_ref[h]

    @pl.when(jb == 0)
    def _():
        kf = k_ref[...].astype(F32).reshape(nblk, blk, HEAD_DIM)
        km = jnp.sum(kf, axis=1) * (1.0 / blk)
        kmean_ref[...] = jnp.zeros_like(kmean_ref)
        kmean_ref[0:nblk, :] = km

    q = q_ref[pl.ds(pl.multiple_of(jb * blk, blk), blk), :]
    km = kmean_ref[...]
    km_hi = km.astype(BF16)
    km_lo = (km - km_hi.astype(F32)).astype(BF16)
    gate = _dot_nt(q, km_hi) + _dot_nt(q, km_lo)

    lane = lax.broadcasted_iota(jnp.int32, gate.shape, 1)
    eligible = lane < jb
    ge = jnp.where(eligible, gate, NEG_INF)
    rank = jnp.zeros(gate.shape, jnp.int32)
    for mblk in range(nblk):
        col = ge[:, mblk:mblk + 1]
        ahead = (col > ge) | ((col == ge) & (mblk < lane))
        rank = rank + ahead.astype(jnp.int32)
    sel = jnp.where(eligible & (rank < TOPK), 1.0, 0.0)

    row = lax.broadcasted_iota(jnp.int32, (blk, blk), 0)
    colk = lax.broadcasted_iota(jnp.int32, (blk, blk), 1)
    rel = row - colk

    k_own = k_ref[pl.ds(pl.multiple_of(jb * blk, blk), blk), :]
    s_own = _dot_nt(q, k_own) * SCALE - slope * rel.astype(F32)
    s_own = jnp.where(rel >= 0, s_own, NEG_INF)
    s_ref[jb] = s_own
    m_ref[...] = s_own

    def scores(n, carry):
        kn = k_ref[pl.ds(pl.multiple_of(n * blk, blk), blk), :]
        dist = (rel + (jb - n) * blk).astype(F32)
        s = _dot_nt(q, kn) * SCALE - slope * dist
        keep = jnp.sum(jnp.where(lane == n, sel, 0.0), axis=1, keepdims=True) > 0.0
        s = jnp.where(keep, s, NEG_INF)
        s_ref[n] = s
        m_ref[...] = jnp.maximum(m_ref[...], s)
        return carry

    lax.fori_loop(0, jb, scores, 0)
    m = jnp.max(m_ref[...], axis=1, keepdims=True)

    l_ref[...] = jnp.zeros_like(l_ref)
    acc_ref[...] = jnp.zeros_like(acc_ref)

    def weighted(n, carry):
        p = jnp.exp(s_ref[n] - m)
        l_ref[...] += p
        acc_ref[...] += _dot(p.astype(BF16), v_ref[pl.ds(pl.multiple_of(n * blk, blk), blk), :])
        return carry

    lax.fori_loop(0, jb + 1, weighted, 0)
    denom = jnp.sum(l_ref[...], axis=1, keepdims=True)
    o_ref[...] = (acc_ref[...] / denom * za_ref[...].astype(F32)).astype(o_ref.dtype)


def _moba_prompt(act, slopes, batch, seq):
    blk = MOBA_BLOCK
    nblk = seq // blk
    full = lambda c0: pl.BlockSpec((seq, HEAD_DIM), lambda b, h, j: (b, c0 * (COL // HEAD_DIM) + h))
    tile = lambda c0: pl.BlockSpec((blk, HEAD_DIM), lambda b, h, j: (b * nblk + j, c0 * (COL // HEAD_DIM) + h))
    return pl.pallas_call(
        _moba_prompt_kernel,
        grid=(batch, N_HEADS, nblk),
        in_specs=[
            pl.BlockSpec(memory_space=pltpu.SMEM),
            full(C_Q),
            full(C_K),
            full(C_V),
            tile(C_ZA),
        ],
        out_specs=pl.BlockSpec((blk, HEAD_DIM), lambda b, h, j: (b * nblk + j, h)),
        out_shape=jax.ShapeDtypeStruct((batch * seq, D_ATT), BF16),
        scratch_shapes=[
            pltpu.VMEM((128, HEAD_DIM), F32),
            pltpu.VMEM((nblk, blk, blk), F32),
            pltpu.VMEM((blk, blk), F32),
            pltpu.VMEM((blk, blk), F32),
            pltpu.VMEM((blk, HEAD_DIM), F32),
        ],
        compiler_params=_params("arbitrary", "arbitrary", "arbitrary"),
        name="moba_prompt",
    )(slopes, act, act, act, act)


def _sgu_prompt_kernel(ws_ref, bs_ref, vn_ref, u_ref, zb_ref, o_ref):
    row = lax.broadcasted_iota(jnp.int32, (CHUNK, CHUNK), 0)
    col = lax.broadcasted_iota(jnp.int32, (CHUNK, CHUNK), 1)
    ws = jnp.where(row >= col, ws_ref[...], 0.0).astype(BF16)
    bs = bs_ref[...]
    for c in range(vn_ref.shape[0] // CHUNK):
        r = slice(c * CHUNK, (c + 1) * CHUNK)
        mix = _dot(ws, vn_ref[r, :]) + bs
        o_ref[r, :] = (u_ref[r, :].astype(F32) * mix * zb_ref[r, :].astype(F32)).astype(o_ref.dtype)


def _sgu_prompt(act, w_s, b_s, tm):
    rows = act.shape[0]
    per = COL // GROUP_DIM
    col = lambda c0: pl.BlockSpec((tm, GROUP_DIM), lambda m, g: (m, c0 * per + g))
    return pl.pallas_call(
        _sgu_prompt_kernel,
        grid=(rows // tm, N_GROUPS),
        in_specs=[
            pl.BlockSpec((None, CHUNK, CHUNK), lambda m, g: (g, 0, 0)),
            pl.BlockSpec((None, CHUNK, 1), lambda m, g: (g, 0, 0)),
            col(C_VB),
            col(C_U),
            col(C_ZB),
        ],
        out_specs=pl.BlockSpec((tm, GROUP_DIM), lambda m, g: (m, g)),
        out_shape=jax.ShapeDtypeStruct((rows, D_SGU), BF16),
        compiler_params=_params("arbitrary", "arbitrary"),
        name="sgu_prompt",
    )(w_s, b_s.reshape(N_GROUPS, CHUNK, 1), act, act, act)


def _outproj_kernel(a_ref, b_ref, ga0_ref, ga1_ref, gb0_ref, gb1_ref, x_ref, gate_ref, wpa_ref, wpb_ref, wout_ref,
                    o_ref):
    ya = _dot(a_ref[...].astype(BF16), wpa_ref[...])
    yb = _dot(b_ref[...].astype(BF16), wpb_ref[...])
    sga = jnp.concatenate([ga0_ref[...], ga1_ref[...]], axis=1).astype(F32)
    sgb = jnp.concatenate([gb0_ref[...], gb1_ref[...]], axis=1).astype(F32)
    mix = sga * ya + sgb * yb
    r = _dot(mix.astype(BF16), wout_ref[...])
    o_ref[...] = x_ref[...] + gate_ref[...] * r


def _outproj(a, b, act, x, gate, w_pa, w_pb, w_out, tm, rows_per_batch):
    rows, d = x.shape
    gate, gate_spec = _row_vec_spec(gate, tm, rows_per_batch)
    resident = lambda w: pl.BlockSpec(w.shape, lambda m: (0, 0), pipeline_mode=pl.Buffered(1))
    col = lambda c: pl.BlockSpec((tm, COL), lambda m: (m, c))
    return pl.pallas_call(
        _outproj_kernel,
        grid=(rows // tm,),
        in_specs=[
            pl.BlockSpec((tm, D_ATT), lambda m: (m, 0)),
            pl.BlockSpec((tm, D_SGU), lambda m: (m, 0)),
            col(C_GA),
            col(C_GA + 1),
            col(C_GB),
            col(C_GB + 1),
            pl.BlockSpec((tm, d), lambda m: (m, 0)),
            gate_spec,
            resident(w_pa),
            resident(w_pb),
            resident(w_out),
        ],
        out_specs=pl.BlockSpec((tm, d), lambda m: (m, 0)),
        out_shape=jax.ShapeDtypeStruct((rows, d), F32),
        compiler_params=_params("arbitrary"),
        name="out_proj",
    )(a, b, act, act, act, act, x, gate, w_pa, w_pb, w_out)


PAGES_PER_STEP = 8


def _kmean_kernel(pt_ref, *refs):
    pages, o_ref = refs[:PAGES_PER_STEP], refs[PAGES_PER_STEP]
    ppb = MOBA_BLOCK // pages[0].shape[0]
    for i in range(PAGES_PER_STEP // ppb):
        tot = pages[i * ppb][...].sum(axis=0)
        for j in range(1, ppb):
            tot = tot + pages[i * ppb + j][...].sum(axis=0)
        o_ref[i] = tot * (1.0 / MOBA_BLOCK)


def _kmean(cache_k, page_table):
    depth, _, page, nh, hd = cache_k.shape
    b, n_pages = page_table.shape
    ppb = MOBA_BLOCK // page
    steps = n_pages // PAGES_PER_STEP
    bps = PAGES_PER_STEP // ppb

    def page_spec(j):
        return pl.BlockSpec((None, None, page, nh, hd),
                            lambda l, bi, i, pt: (l, pt[bi * n_pages + i * PAGES_PER_STEP + j], 0, 0, 0))

    return pl.pallas_call(
        _kmean_kernel,
        grid_spec=pltpu.PrefetchScalarGridSpec(
            num_scalar_prefetch=1,
            grid=(depth, b, steps),
            in_specs=[page_spec(j) for j in range(PAGES_PER_STEP)],
            out_specs=pl.BlockSpec((None, None, bps, nh, hd), lambda l, bi, i, pt: (l, bi, i, 0, 0)),
        ),
        out_shape=jax.ShapeDtypeStruct((depth, b, n_pages // ppb, nh, hd), F32),
        compiler_params=_params("arbitrary", "arbitrary", "arbitrary"),
        name="kmean_pages",
    )(page_table.reshape(-1), *([cache_k] * PAGES_PER_STEP))


def _select_kernel(q_ref, km_ref, sel_ref):
    q = q_ref[...].astype(F32)
    km = km_ref[...]
    g = jnp.sum(km * q[None], axis=-1, keepdims=True)
    nb = g.shape[0]
    idx = lax.broadcasted_iota(jnp.int32, g.shape, 0)
    for r in range(TOPK):
        best = jnp.max(g, axis=0, keepdims=True)
        pick = jnp.min(jnp.where(g == best, idx, nb), axis=0, keepdims=True)
        sel_ref[r:r + 1] = pick
        g = jnp.where(idx == pick, NEG_INF, g)


def _select(q, kmean_l):
    b, nh, hd = q.shape
    nb = kmean_l.shape[1]
    sel = pl.pallas_call(
        _select_kernel,
        grid=(b,),
        in_specs=[
            pl.BlockSpec((None, nh, hd), lambda i: (i, 0, 0)),
            pl.BlockSpec((None, nb, nh, hd), lambda i: (i, 0, 0, 0)),
        ],
        out_specs=pl.BlockSpec((None, TOPK, nh, 1), lambda i: (i, 0, 0, 0)),
        out_shape=jax.ShapeDtypeStruct((b, TOPK, nh, 1), jnp.int32),
        compiler_params=_params("arbitrary"),
        name="moba_select",
    )(q, kmean_l)
    return sel.reshape(b, TOPK, nh)


def _moba_sample_kernel(sel_ref, pt_ref, slopes_ref, q_ref, kn_ref, vn_ref, za_ref, ck_ref, cv_ref, o_ref,
                        kbuf, vbuf, sem, *, layer, past, page, n_pages):
    nh = q_ref.shape[0]
    ppb = MOBA_BLOCK // page
    ntile = TOPK * ppb
    b = pl.program_id(0)

    def block_of(h, t):
        return sel_ref[(b * TOPK + t // ppb) * nh + h]

    def tile_copies(h, t):
        pg = pt_ref[b * n_pages + block_of(h, t) * ppb + t % ppb]
        return (pltpu.make_async_copy(ck_ref.at[layer, pg, :, h, :], kbuf.at[h, t], sem.at[0, h, t]),
                pltpu.make_async_copy(cv_ref.at[layer, pg, :, h, :], vbuf.at[h, t], sem.at[1, h, t]))

    for h in range(nh):
        for t in range(ntile):
            for cp in tile_copies(h, t):
                cp.start()

    pos = lax.broadcasted_iota(jnp.int32, (page, 1), 0)
    for h in range(nh):
        slope = slopes_ref[h]
        q = q_ref[h]
        s_self = jnp.sum(kn_ref[h] * q, axis=1, keepdims=True) * SCALE
        scores = []
        m = s_self
        for t in range(ntile):
            k_cp, v_cp = tile_copies(h, t)
            k_cp.wait()
            v_cp.wait()
            dist = (past - (block_of(h, t) * MOBA_BLOCK + (t % ppb) * page) - pos).astype(F32)
            s = jnp.sum(kbuf[h, t] * q, axis=1, keepdims=True) * SCALE - slope * dist
            scores.append(s)
            m = jnp.maximum(m, jnp.max(s, axis=0, keepdims=True))
        p_self = jnp.exp(s_self - m)
        denom = p_self
        out = p_self * vn_ref[h]
        for t in range(ntile):
            p = jnp.exp(scores[t] - m)
            denom = denom + jnp.sum(p, axis=0, keepdims=True)
            out = out + jnp.sum(p * vbuf[h, t], axis=0, keepdims=True)
        o_ref[h] = out / denom * za_ref[h]


def _moba_sample(sel, page_table, slopes, q, k_new, v_new, za, cache_k, cache_v, layer):
    b, nh, _, hd = q.shape
    page = cache_k.shape[2]
    n_pages = page_table.shape[1]
    past = n_pages * page
    assert past % MOBA_BLOCK == 0 and past // MOBA_BLOCK >= TOPK
    ntile = TOPK * (MOBA_BLOCK // page)
    vec = pl.BlockSpec((None, nh, 1, hd), lambda bi, s, p: (bi, 0, 0, 0))
    hbm = pl.BlockSpec(memory_space=pl.ANY)
    return pl.pallas_call(
        functools.partial(_moba_sample_kernel, layer=layer, past=past, page=page, n_pages=n_pages),
        grid_spec=pltpu.PrefetchScalarGridSpec(
            num_scalar_prefetch=2,
            grid=(b,),
            in_specs=[pl.BlockSpec(memory_space=pltpu.SMEM), vec, vec, vec, vec, hbm, hbm],
            out_specs=vec,
            scratch_shapes=[
                pltpu.VMEM((nh, ntile, page, hd), F32),
                pltpu.VMEM((nh, ntile, page, hd), F32),
                pltpu.SemaphoreType.DMA((2, nh, ntile)),
            ],
        ),
        out_shape=jax.ShapeDtypeStruct((b, nh, 1, hd), F32),
        compiler_params=_params("arbitrary"),
        name="moba_sample",
    )(sel.reshape(-1), page_table.reshape(-1), slopes, q, k_new, v_new, za, cache_k, cache_v)


def _sgu_sample_kernel(u_ref, vn_ref, zb_ref, w0_ref, b0_ref, o_ref):
    mix = w0_ref[...] * vn_ref[...] + b0_ref[...]
    o_ref[...] = (u_ref[...].astype(F32) * mix * zb_ref[...].astype(F32)).astype(o_ref.dtype)


def _sgu_sample(act, vn, w0, b0):
    rows = act.shape[0]
    blk = lambda c: pl.BlockSpec((rows, COL), lambda i: (0, c))
    vec = pl.BlockSpec((1, COL), lambda i: (0, 0))
    return pl.pallas_call(
        _sgu_sample_kernel,
        grid=(1,),
        in_specs=[blk(C_U), blk(0), blk(C_ZB), vec, vec],
        out_specs=blk(0),
        out_shape=jax.ShapeDtypeStruct((rows, D_SGU), BF16),
        compiler_params=_params("arbitrary"),
        name="sgu_sample",
    )(act, vn, act, w0, b0)


def kernel(x_prompt, x_sample, cache_k, cache_v, page_table, c_prompt, c_sample, norm_g, w_ada, b_ada, w_in,
           sgu_ln_g, sgu_ln_b, w_s, b_s, w_proj_a, w_proj_b, w_out, final_g):
    batch, seq, d = x_prompt.shape
    dec_batch, dec_seq, _ = x_sample.shape
    assert dec_seq == 1
    depth = w_in.shape[0]
    slopes = 2.0 ** (-8.0 * jnp.arange(1, N_HEADS + 1, dtype=F32) / N_HEADS)

    mod = _mod(jnp.concatenate([c_prompt, c_sample], axis=0), w_ada, b_ada)
    kmean = _kmean(cache_k, page_table)

    xp = x_prompt.reshape(batch * seq, d)
    xs = x_sample.reshape(dec_batch, d)
    tm_p, tm_s = 512, dec_batch
    kp_l, vp_l, ks_l, vs_l, us_l = [], [], [], [], []
    for l in range(depth):
        w_pa, w_pb, w_o = w_proj_a[l].astype(BF16), w_proj_b[l].astype(BF16), w_out[l].astype(BF16)
        shift, scale, gate = mod[l, :, :d], mod[l, :, d:2 * d], mod[l, :, 2 * d:]

        hp = _norm_mod(xp, norm_g[l], scale[:batch], shift[:batch], tm_p, seq, BF16)
        act, kp, vp, _ = _inproj(hp, w_in[l].astype(BF16), sgu_ln_g[l], sgu_ln_b[l], tm_p)
        att = _moba_prompt(act, slopes, batch, seq)
        sgu = _sgu_prompt(act, w_s[l], b_s[l], tm_p)
        xp = _outproj(att, sgu, act, xp, gate[:batch], w_pa, w_pb, w_o, 256, seq)
        kp_l.append(kp.reshape(batch, seq, N_HEADS, HEAD_DIM))
        vp_l.append(vp.reshape(batch, seq, N_HEADS, HEAD_DIM))

        hs = _norm_mod(xs, norm_g[l], scale[batch:], shift[batch:], tm_s, 1, F32)
        act_s, ks, vs, us = _inproj(hs, w_in[l], sgu_ln_g[l], sgu_ln_b[l], tm_s)
        heads = lambda a: a.reshape(dec_batch, N_HEADS, 1, HEAD_DIM)
        q_s = act_s[:, C_Q * COL:(C_Q + 1) * COL]
        za_s = act_s[:, C_ZA * COL:(C_ZA + 1) * COL]
        sel = _select(q_s.reshape(dec_batch, N_HEADS, HEAD_DIM), kmean[l])
        att_s = _moba_sample(sel, page_table, slopes, heads(q_s), heads(ks), heads(vs), heads(za_s),
                             cache_k, cache_v, l)
        w0 = jnp.repeat(w_s[l, :, 0, 0], GROUP_DIM).reshape(1, D_SGU)
        b0 = jnp.repeat(b_s[l, :, 0], GROUP_DIM).reshape(1, D_SGU)
        sgu_s = _sgu_sample(act_s, us, w0, b0)
        xs = _outproj(att_s.reshape(dec_batch, D_ATT), sgu_s, act_s, xs, gate[batch:], w_pa, w_pb, w_o, tm_s, 1)
        ks_l.append(ks.reshape(dec_batch, 1, N_HEADS, HEAD_DIM))
        vs_l.append(vs.reshape(dec_batch, 1, N_HEADS, HEAD_DIM))
        us_l.append(us.reshape(dec_batch, 1, D_SGU))

    y_prompt = _final_norm(xp, final_g, tm_p).reshape(batch, seq, d)
    y_sample = _final_norm(xs, final_g, dec_batch).reshape(dec_batch, 1, d)
    return (y_prompt, y_sample, jnp.stack(kp_l), jnp.stack(vp_l), jnp.stack(ks_l), jnp.stack(vs_l), jnp.stack(us_l))
```

```python
import functools

import numpy as np
import jax
import jax.numpy as jnp
from jax import lax
from jax.experimental import pallas as pl
from jax.experimental.pallas import tpu as pltpu

D_MODEL = 2048
HEAD_DIM = 128
D_ATT = D_MODEL // 2
N_HEADS = D_ATT // HEAD_DIM
GROUP_DIM = 128
D_SGU = D_MODEL // 2
N_GROUPS = D_SGU // GROUP_DIM
MOBA_BLOCK = 256
TOPK = 3
CHUNK = 128
EPS = 1e-6
SCALE = HEAD_DIM ** -0.5
N_IN = 4 * D_ATT + 3 * D_SGU + 2 * D_MODEL

COL = 1024
N_COLS = N_IN // COL
C_Q, C_K, C_V, C_ZA, C_U, C_VB, C_ZB, C_GA, C_GB = 0, 1, 2, 3, 4, 5, 6, 7, 9

VMEM_LIMIT = 56 * 1024 * 1024
F32 = jnp.float32
BF16 = jnp.bfloat16
NEG_INF = float("-inf")


def _params(*sem):
    return pltpu.CompilerParams(dimension_semantics=sem, vmem_limit_bytes=VMEM_LIMIT)


def _dot(a, b):
    precision = lax.Precision.HIGHEST if a.dtype == F32 else None
    return jnp.dot(a, b, preferred_element_type=F32, precision=precision)


def _gelu(x):
    return 0.5 * x * (1.0 + lax.erf(x * (2.0 ** -0.5)))


def _dot_nt(a, b):
    return lax.dot_general(a, b, (((1,), (1,)), ((), ())), preferred_element_type=F32)


def _mod_kernel(c_ref, w_ref, b_ref, o_ref):
    c = c_ref[...]
    o_ref[...] = _dot(c * jax.nn.sigmoid(c), w_ref[...]) + b_ref[...]


def _mod(c_all, w_ada, b_ada):
    depth, d, n3 = w_ada.shape
    rows = c_all.shape[0]
    tn = 768
    return pl.pallas_call(
        _mod_kernel,
        grid=(depth, n3 // tn),
        in_specs=[
            pl.BlockSpec((rows, d), lambda l, n: (0, 0)),
            pl.BlockSpec((None, d, tn), lambda l, n: (l, 0, n)),
            pl.BlockSpec((None, 1, tn), lambda l, n: (l, 0, n)),
        ],
        out_specs=pl.BlockSpec((None, rows, tn), lambda l, n: (l, 0, n)),
        out_shape=jax.ShapeDtypeStruct((depth, rows, n3), F32),
        compiler_params=_params("arbitrary", "arbitrary"),
        name="adaln_mod",
    )(c_all, w_ada, b_ada.reshape(depth, 1, n3))


def _norm_kernel(x_ref, g_ref, scale_ref, shift_ref, o_ref):
    x = x_ref[...]
    y = x * lax.rsqrt(jnp.mean(x * x, axis=-1, keepdims=True) + EPS) * g_ref[...]
    o_ref[...] = (y * (1.0 + scale_ref[...]) + shift_ref[...]).astype(o_ref.dtype)


def _final_norm_kernel(x_ref, g_ref, o_ref):
    x = x_ref[...]
    o_ref[...] = x * lax.rsqrt(jnp.mean(x * x, axis=-1, keepdims=True) + EPS) * g_ref[...]


def _row_vec_spec(vec, tm, rows_per_batch):
    d = vec.shape[-1]
    if rows_per_batch == 1:
        return vec, pl.BlockSpec((tm, d), lambda m: (m, 0))
    assert rows_per_batch % tm == 0
    per = rows_per_batch // tm
    return vec.reshape(vec.shape[0], 1, d), pl.BlockSpec((None, 1, d), lambda m: (m // per, 0, 0))


def _norm_mod(x, g_all, layer, scale, shift, tm, rows_per_batch, out_dtype):
    rows, d = x.shape
    scale, scale_spec = _row_vec_spec(scale, tm, rows_per_batch)
    shift, shift_spec = _row_vec_spec(shift, tm, rows_per_batch)
    return pl.pallas_call(
        _norm_kernel,
        grid=(rows // tm,),
        in_specs=[
            pl.BlockSpec((tm, d), lambda m: (m, 0)),
            pl.BlockSpec((None, 1, d), lambda m: (layer, 0, 0)),
            scale_spec,
            shift_spec,
        ],
        out_specs=pl.BlockSpec((tm, d), lambda m: (m, 0)),
        out_shape=jax.ShapeDtypeStruct((rows, d), out_dtype),
        compiler_params=_params("arbitrary"),
        name="norm_mod",
    )(x, g_all.reshape(-1, 1, d), scale, shift)


def _final_norm(x, g, tm):
    rows, d = x.shape
    return pl.pallas_call(
        _final_norm_kernel,
        grid=(rows // tm,),
        in_specs=[pl.BlockSpec((tm, d), lambda m: (m, 0)), pl.BlockSpec((1, d), lambda m: (0, 0))],
        out_specs=pl.BlockSpec((tm, d), lambda m: (m, 0)),
        out_shape=jax.ShapeDtypeStruct((rows, d), F32),
        compiler_params=_params("arbitrary"),
        name="final_norm",
    )(x, g.reshape(1, d))


INPROJ_CHUNK = 256


def _inproj_kernel(h_ref, w_ref, lng_ref, lnb_ref, act_ref, side_ref, *, emit_vn):
    n = pl.program_id(1)
    tm = h_ref.shape[0]
    chunk = min(INPROJ_CHUNK, tm)

    def column_block(epilogue):
        for c in range(tm // chunk):
            rows = pl.ds(c * chunk, chunk)
            acc = _dot(h_ref[rows, :], w_ref[...])
            act_ref[rows, :] = epilogue(rows, acc).astype(act_ref.dtype)

    def to_side(rows, acc):
        side_ref[rows, :] = acc
        return acc

    def norm_gelu(rows, acc):
        a = _gelu(acc)
        mu = jnp.mean(a, axis=-1, keepdims=True)
        c = a - mu
        var = jnp.mean(c * c, axis=-1, keepdims=True)
        vn = c * lax.rsqrt(var + EPS) * lng_ref[...] + lnb_ref[...]
        return to_side(rows, vn) if emit_vn else vn

    @pl.when(n >= C_GA)
    def _():
        column_block(lambda rows, acc: jax.nn.sigmoid(acc))

    @pl.when(n == C_Q)
    def _():
        column_block(lambda rows, acc: acc * SCALE)

    @pl.when((n == C_K) | (n == C_V))
    def _():
        column_block(to_side)

    @pl.when((n == C_ZA) | (n == C_ZB))
    def _():
        column_block(lambda rows, acc: jax.nn.silu(acc))

    @pl.when(n == C_U)
    def _():
        column_block(lambda rows, acc: _gelu(acc))

    @pl.when(n == C_VB)
    def _():
        column_block(norm_gelu)


def _inproj(h, w_all, ln_g, ln_b, layer, tm, emit_vn):
    rows, d = h.shape
    assert h.dtype == w_all.dtype
    n_side = 3 if emit_vn else 2

    def side_index(m, n):
        blk = jnp.where(n >= C_V, 1, 0)
        return (m, blk + jnp.where(n >= C_VB, 1, 0) if emit_vn else blk)

    vec = pl.BlockSpec((None, 1, COL), lambda m, n: (layer, 0, 0))
    return pl.pallas_call(
        functools.partial(_inproj_kernel, emit_vn=emit_vn),
        grid=(rows // tm, N_COLS),
        in_specs=[
            pl.BlockSpec((tm, d), lambda m, n: (m, 0)),
            pl.BlockSpec((None, d, COL), lambda m, n: (layer, 0, n)),
            vec,
            vec,
        ],
        out_specs=[pl.BlockSpec((tm, COL), lambda m, n: (m, n)), pl.BlockSpec((tm, COL), side_index)],
        out_shape=[jax.ShapeDtypeStruct((rows, N_IN), h.dtype), jax.ShapeDtypeStruct((rows, n_side * COL), F32)],
        compiler_params=_params("arbitrary", "arbitrary"),
        name="in_proj",
    )(h, w_all, ln_g.reshape(-1, 1, COL), ln_b.reshape(-1, 1, COL))


SEL_W = 8
MASKED = -1e30


def _slope_parts():
    slopes = (2.0 ** (-8.0 * np.arange(1, N_HEADS + 1, dtype=np.float32) / N_HEADS)).astype(np.float32)
    parts, rest = [], slopes
    while rest.any():
        part = rest.astype(BF16).astype(np.float32)
        parts.append(part)
        rest = rest - part
    return slopes, np.stack(parts)


def _feature_tables(seq):
    _, parts = _slope_parts()
    n_parts = parts.shape[0]
    width = SEL_W + 4 * n_parts
    q_rows = -(-width // 16) * 16 - SEL_W
    assert seq // MOBA_BLOCK <= SEL_W and width <= HEAD_DIM
    pos = jnp.arange(seq, dtype=jnp.int32)
    blk_of = pos // MOBA_BLOCK
    a = (blk_of * MOBA_BLOCK).astype(F32)
    r = (pos % MOBA_BLOCK).astype(F32)
    k_cols = [(blk_of == m).astype(F32) for m in range(SEL_W)]
    zeros = jnp.zeros((seq,), F32)
    kfeat, qfeat = [], []
    for h in range(N_HEADS):
        kc, qr = list(k_cols), []
        for p in range(n_parts):
            s = float(parts[p, h])
            kc += [zeros - s, zeros - s, a, r]
            qr += [a, r, zeros + s, zeros + s]
        kc += [zeros] * (HEAD_DIM - len(kc))
        qr += [zeros] * (q_rows - len(qr))
        kfeat.append(jnp.stack(kc, axis=1))
        qfeat.append(jnp.stack(qr, axis=0))
    return jnp.stack(kfeat).astype(BF16), jnp.stack(qfeat)


def _moba_prompt_kernel(q_ref, k_ref, v_ref, za_ref, kfeat_ref, qfeat_ref, o_ref, qa_ref, ka_ref, va_ref, s_ref):
    blk = MOBA_BLOCK
    seq = k_ref.shape[0]
    nblk = seq // blk
    hd = HEAD_DIM

    @pl.when((pl.program_id(0) == 0) & (pl.program_id(1) == 0))
    def _():
        lane = lax.broadcasted_iota(jnp.int32, (seq, hd), 1)
        va_ref[:, hd:] = jnp.where(lane == 0, 1.0, 0.0).astype(BF16)

    q = q_ref[...]
    k = k_ref[...]
    qa_ref[:, :hd] = q
    ka_ref[:, :hd] = k
    ka_ref[:, hd:] = kfeat_ref[...]
    va_ref[:, :hd] = v_ref[...]

    km = jnp.sum(k.astype(F32).reshape(nblk, blk, hd), axis=1) * (1.0 / blk)
    if nblk < SEL_W:
        km = jnp.concatenate([km, jnp.zeros((SEL_W - nblk, hd), F32)], axis=0)
    km_hi = km.astype(BF16).astype(F32)
    gt = _dot_nt(jnp.concatenate([km_hi, km - km_hi], axis=0).astype(BF16), q)
    gate = gt[:SEL_W] + gt[SEL_W:]

    blk_row = lax.broadcasted_iota(jnp.int32, gate.shape, 0)
    blk_own = lax.broadcasted_iota(jnp.int32, gate.shape, 1) // blk
    eligible = blk_row < blk_own
    ge = jnp.where(eligible, gate, NEG_INF)
    rank = jnp.zeros(gate.shape, jnp.int32)
    for m in range(nblk):
        other = ge[m:m + 1, :]
        ahead = (other > ge) | ((other == ge) & (m < blk_row))
        rank = rank + ahead.astype(jnp.int32)
    allowed = (eligible & (rank < TOPK)) | (blk_row == blk_own)
    feat_t = jnp.concatenate([jnp.where(allowed, 0.0, MASKED), qfeat_ref[...]], axis=0).astype(BF16)

    eye = (lax.broadcasted_iota(jnp.int32, (blk, blk), 0) == lax.broadcasted_iota(jnp.int32, (blk, blk), 1))
    eye = jnp.where(eye, 1.0, 0.0).astype(BF16)
    pad = jnp.zeros((hd - feat_t.shape[0], blk), BF16)
    for t in range(nblk):
        rows = slice(t * blk, (t + 1) * blk)
        qa_ref[rows, hd:] = _dot_nt(eye, jnp.concatenate([feat_t[:, rows], pad], axis=0)).astype(BF16)

    causal = lax.broadcasted_iota(jnp.int32, (blk, blk), 0) >= lax.broadcasted_iota(jnp.int32, (blk, blk), 1)
    for t in range(nblk):
        rows = slice(t * blk, (t + 1) * blk)
        qt = qa_ref[rows, :]
        for n in range(t + 1):
            s = _dot_nt(qt, ka_ref[n * blk:(n + 1) * blk, :])
            if n == t:
                s = jnp.where(causal, s, MASKED)
            s_ref[n] = s
        m_el = s_ref[0]
        for n in range(1, t + 1):
            m_el = jnp.maximum(m_el, s_ref[n])
        m = jnp.max(m_el, axis=1, keepdims=True)
        acc = None
        for n in range(t + 1):
            p = jnp.exp(s_ref[n] - m).astype(BF16)
            pv = _dot(p, va_ref[n * blk:(n + 1) * blk, :])
            acc = pv if acc is None else acc + pv
        out = acc[:, :hd] / acc[:, hd:hd + 1] * za_ref[rows, :].astype(F32)
        o_ref[rows, :] = out.astype(o_ref.dtype)


def _moba_prompt(act, batch, seq):
    kfeat, qfeat = _feature_tables(seq)
    per = COL // HEAD_DIM
    head = lambda c0: pl.BlockSpec((seq, HEAD_DIM), lambda b, h: (b, c0 * per + h))
    return pl.pallas_call(
        _moba_prompt_kernel,
        grid=(batch, N_HEADS),
        in_specs=[
            head(C_Q),
            head(C_K),
            head(C_V),
            head(C_ZA),
            pl.BlockSpec((None, seq, HEAD_DIM), lambda b, h: (h, 0, 0)),
            pl.BlockSpec((None, qfeat.shape[1], seq), lambda b, h: (h, 0, 0)),
        ],
        out_specs=pl.BlockSpec((seq, HEAD_DIM), lambda b, h: (b, h)),
        out_shape=jax.ShapeDtypeStruct((batch * seq, D_ATT), BF16),
        scratch_shapes=[
            pltpu.VMEM((seq, 2 * HEAD_DIM), BF16),
            pltpu.VMEM((seq, 2 * HEAD_DIM), BF16),
            pltpu.VMEM((seq, 2 * HEAD_DIM), BF16),
            pltpu.VMEM((seq // MOBA_BLOCK, MOBA_BLOCK, MOBA_BLOCK), F32),
        ],
        compiler_params=_params("arbitrary", "arbitrary"),
        name="moba_prompt",
    )(act, act, act, act, kfeat, qfeat)


def _sgu_prompt_kernel(ws_ref, bs_ref, vn_ref, u_ref, zb_ref, o_ref):
    row = lax.broadcasted_iota(jnp.int32, (CHUNK, CHUNK), 0)
    col = lax.broadcasted_iota(jnp.int32, (CHUNK, CHUNK), 1)
    ws = jnp.where(row >= col, ws_ref[...], 0.0).astype(BF16)
    bs = bs_ref[...]
    for c in range(vn_ref.shape[0] // CHUNK):
        r = slice(c * CHUNK, (c + 1) * CHUNK)
        mix = _dot(ws, vn_ref[r, :]) + bs
        o_ref[r, :] = (u_ref[r, :].astype(F32) * mix * zb_ref[r, :].astype(F32)).astype(o_ref.dtype)


def _sgu_prompt(act, w_s, b_s, layer, tm):
    rows = act.shape[0]
    per = COL // GROUP_DIM
    col = lambda c0: pl.BlockSpec((tm, GROUP_DIM), lambda m, g: (m, c0 * per + g))
    return pl.pallas_call(
        _sgu_prompt_kernel,
        grid=(rows // tm, N_GROUPS),
        in_specs=[
            pl.BlockSpec((None, None, CHUNK, CHUNK), lambda m, g: (layer, g, 0, 0)),
            pl.BlockSpec((None, None, CHUNK, 1), lambda m, g: (layer, g, 0, 0)),
            col(C_VB),
            col(C_U),
            col(C_ZB),
        ],
        out_specs=pl.BlockSpec((tm, GROUP_DIM), lambda m, g: (m, g)),
        out_shape=jax.ShapeDtypeStruct((rows, D_SGU), BF16),
        compiler_params=_params("arbitrary", "arbitrary"),
        name="sgu_prompt",
    )(w_s, b_s.reshape(-1, N_GROUPS, CHUNK, 1), act, act, act)


def _outproj_kernel(a_ref, b_ref, ga0_ref, ga1_ref, gb0_ref, gb1_ref, x_ref, gate_ref, wpa_ref, wpb_ref, wout_ref,
                    o_ref):
    ya = _dot(a_ref[...].astype(BF16), wpa_ref[...])
    yb = _dot(b_ref[...].astype(BF16), wpb_ref[...])
    sga = jnp.concatenate([ga0_ref[...], ga1_ref[...]], axis=1).astype(F32)
    sgb = jnp.concatenate([gb0_ref[...], gb1_ref[...]], axis=1).astype(F32)
    mix = sga * ya + sgb * yb
    r = _dot(mix.astype(BF16), wout_ref[...])
    o_ref[...] = x_ref[...] + gate_ref[...] * r


def _outproj(a, b, act, x, gate, w_pa, w_pb, w_out, layer, tm, rows_per_batch):
    rows, d = x.shape
    gate, gate_spec = _row_vec_spec(gate, tm, rows_per_batch)
    resident = lambda w: pl.BlockSpec((None,) + w.shape[1:], lambda m: (layer, 0, 0), pipeline_mode=pl.Buffered(1))
    col = lambda c: pl.BlockSpec((tm, COL), lambda m: (m, c))
    return pl.pallas_call(
        _outproj_kernel,
        grid=(rows // tm,),
        in_specs=[
            pl.BlockSpec((tm, D_ATT), lambda m: (m, 0)),
            pl.BlockSpec((tm, D_SGU), lambda m: (m, 0)),
            col(C_GA),
            col(C_GA + 1),
            col(C_GB),
            col(C_GB + 1),
            pl.BlockSpec((tm, d), lambda m: (m, 0)),
            gate_spec,
            resident(w_pa),
            resident(w_pb),
            resident(w_out),
        ],
        out_specs=pl.BlockSpec((tm, d), lambda m: (m, 0)),
        out_shape=jax.ShapeDtypeStruct((rows, d), F32),
        compiler_params=_params("arbitrary"),
        name="out_proj",
    )(a, b, act, act, act, act, x, gate, w_pa, w_pb, w_out)


PAGES_PER_STEP = 8


def _kmean_kernel(pt_ref, *refs):
    pages, o_ref = refs[:PAGES_PER_STEP], refs[PAGES_PER_STEP]
    ppb = MOBA_BLOCK // pages[0].shape[0]
    for i in range(PAGES_PER_STEP // ppb):
        tot = pages[i * ppb][...].sum(axis=0)
        for j in range(1, ppb):
            tot = tot + pages[i * ppb + j][...].sum(axis=0)
        o_ref[i] = tot * (1.0 / MOBA_BLOCK)


def _kmean(cache_k, page_table):
    depth, _, page, nh, hd = cache_k.shape
    b, n_pages = page_table.shape
    ppb = MOBA_BLOCK // page
    steps = n_pages // PAGES_PER_STEP
    bps = PAGES_PER_STEP // ppb

    def page_spec(j):
        return pl.BlockSpec((None, None, page, nh, hd),
                            lambda l, bi, i, pt: (l, pt[bi * n_pages + i * PAGES_PER_STEP + j], 0, 0, 0))

    return pl.pallas_call(
        _kmean_kernel,
        grid_spec=pltpu.PrefetchScalarGridSpec(
            num_scalar_prefetch=1,
            grid=(depth, b, steps),
            in_specs=[page_spec(j) for j in range(PAGES_PER_STEP)],
            out_specs=pl.BlockSpec((None, None, bps, nh, hd), lambda l, bi, i, pt: (l, bi, i, 0, 0)),
        ),
        out_shape=jax.ShapeDtypeStruct((depth, b, n_pages // ppb, nh, hd), F32),
        compiler_params=_params("arbitrary", "arbitrary", "arbitrary"),
        name="kmean_pages",
    )(page_table.reshape(-1), *([cache_k] * PAGES_PER_STEP))


def _select_kernel(q_ref, km_ref, sel_ref):
    q = q_ref[...].astype(F32)
    km = km_ref[...]
    g = jnp.sum(km * q[None], axis=-1, keepdims=True)
    nb = g.shape[0]
    idx = lax.broadcasted_iota(jnp.int32, g.shape, 0)
    for r in range(TOPK):
        best = jnp.max(g, axis=0, keepdims=True)
        pick = jnp.min(jnp.where(g == best, idx, nb), axis=0, keepdims=True)
        sel_ref[r:r + 1] = pick
        g = jnp.where(idx == pick, NEG_INF, g)


def _select(q, kmean, layer):
    b, nh, hd = q.shape
    nb = kmean.shape[2]
    sel = pl.pallas_call(
        _select_kernel,
        grid=(b,),
        in_specs=[
            pl.BlockSpec((None, nh, hd), lambda i: (i, 0, 0)),
            pl.BlockSpec((None, None, nb, nh, hd), lambda i: (layer, i, 0, 0, 0)),
        ],
        out_specs=pl.BlockSpec((None, TOPK, nh, 1), lambda i: (i, 0, 0, 0)),
        out_shape=jax.ShapeDtypeStruct((b, TOPK, nh, 1), jnp.int32),
        compiler_params=_params("arbitrary"),
        name="moba_select",
    )(q, kmean)
    return sel.reshape(b, TOPK, nh)


def _moba_sample_kernel(sel_ref, pt_ref, slopes_ref, q_ref, kn_ref, vn_ref, za_ref, ck_ref, cv_ref, o_ref,
                        kbuf, vbuf, sem, *, layer, past, page, n_pages):
    nh = q_ref.shape[0]
    ppb = MOBA_BLOCK // page
    ntile = TOPK * ppb
    b = pl.program_id(0)

    def block_of(h, t):
        return sel_ref[(b * TOPK + t // ppb) * nh + h]

    def tile_copies(h, t):
        pg = pt_ref[b * n_pages + block_of(h, t) * ppb + t % ppb]
        return (pltpu.make_async_copy(ck_ref.at[layer, pg, :, h, :], kbuf.at[h, t], sem.at[0, h, t]),
                pltpu.make_async_copy(cv_ref.at[layer, pg, :, h, :], vbuf.at[h, t], sem.at[1, h, t]))

    for h in range(nh):
        for t in range(ntile):
            for cp in tile_copies(h, t):
                cp.start()

    pos = lax.broadcasted_iota(jnp.int32, (page, 1), 0)
    for h in range(nh):
        slope = slopes_ref[h]
        q = q_ref[h]
        s_self = jnp.sum(kn_ref[h] * q, axis=1, keepdims=True)
        scores = []
        m = s_self
        for t in range(ntile):
            k_cp, v_cp = tile_copies(h, t)
            k_cp.wait()
            v_cp.wait()
            dist = (past - (block_of(h, t) * MOBA_BLOCK + (t % ppb) * page) - pos).astype(F32)
            s = jnp.sum(kbuf[h, t] * q, axis=1, keepdims=True) - slope * dist
            scores.append(s)
            m = jnp.maximum(m, jnp.max(s, axis=0, keepdims=True))
        p_self = jnp.exp(s_self - m)
        denom = p_self
        out = p_self * vn_ref[h]
        for t in range(ntile):
            p = jnp.exp(scores[t] - m)
            denom = denom + jnp.sum(p, axis=0, keepdims=True)
            out = out + jnp.sum(p * vbuf[h, t], axis=0, keepdims=True)
        o_ref[h] = out / denom * za_ref[h]


def _moba_sample(sel, page_table, slopes, q, k_new, v_new, za, cache_k, cache_v, layer):
    b, nh, _, hd = q.shape
    page = cache_k.shape[2]
    n_pages = page_table.shape[1]
    past = n_pages * page
    assert past % MOBA_BLOCK == 0 and past // MOBA_BLOCK >= TOPK
    ntile = TOPK * (MOBA_BLOCK // page)
    vec = pl.BlockSpec((None, nh, 1, hd), lambda bi, s, p: (bi, 0, 0, 0))
    hbm = pl.BlockSpec(memory_space=pl.ANY)
    return pl.pallas_call(
        functools.partial(_moba_sample_kernel, layer=layer, past=past, page=page, n_pages=n_pages),
        grid_spec=pltpu.PrefetchScalarGridSpec(
            num_scalar_prefetch=2,
            grid=(b,),
            in_specs=[pl.BlockSpec(memory_space=pltpu.SMEM), vec, vec, vec, vec, hbm, hbm],
            out_specs=vec,
            scratch_shapes=[
                pltpu.VMEM((nh, ntile, page, hd), F32),
                pltpu.VMEM((nh, ntile, page, hd), F32),
                pltpu.SemaphoreType.DMA((2, nh, ntile)),
            ],
        ),
        out_shape=jax.ShapeDtypeStruct((b, nh, 1, hd), F32),
        compiler_params=_params("arbitrary"),
        name="moba_sample",
    )(sel.reshape(-1), page_table.reshape(-1), slopes, q, k_new, v_new, za, cache_k, cache_v)


def _sgu_sample_kernel(u_ref, vn_ref, zb_ref, w0_ref, b0_ref, o_ref):
    mix = w0_ref[...] * vn_ref[...] + b0_ref[...]
    o_ref[...] = (u_ref[...].astype(F32) * mix * zb_ref[...].astype(F32)).astype(o_ref.dtype)


def _sgu_sample(act, vn, w0, b0):
    rows = act.shape[0]
    blk = lambda c: pl.BlockSpec((rows, COL), lambda i: (0, c))
    vec = pl.BlockSpec((1, COL), lambda i: (0, 0))
    return pl.pallas_call(
        _sgu_sample_kernel,
        grid=(1,),
        in_specs=[blk(C_U), blk(0), blk(C_ZB), vec, vec],
        out_specs=blk(0),
        out_shape=jax.ShapeDtypeStruct((rows, D_SGU), BF16),
        compiler_params=_params("arbitrary"),
        name="sgu_sample",
    )(act, vn, act, w0, b0)


def kernel(x_prompt, x_sample, cache_k, cache_v, page_table, c_prompt, c_sample, norm_g, w_ada, b_ada, w_in,
           sgu_ln_g, sgu_ln_b, w_s, b_s, w_proj_a, w_proj_b, w_out, final_g):
    batch, seq, d = x_prompt.shape
    dec_batch, dec_seq, _ = x_sample.shape
    assert dec_seq == 1
    depth = w_in.shape[0]
    slopes = jnp.asarray(_slope_parts()[0])

    mod = _mod(jnp.concatenate([c_prompt, c_sample], axis=0), w_ada, b_ada)
    kmean = _kmean(cache_k, page_table)

    xp = x_prompt.reshape(batch * seq, d)
    xs = x_sample.reshape(dec_batch, d)
    tm_norm, tm_in, tm_sgu, tm_out, tm_s = 512, 1024, 512, 256, dec_batch
    w_in_b, w_pa, w_pb, w_o = (w.astype(BF16) for w in (w_in, w_proj_a, w_proj_b, w_out))
    kp_l, vp_l, ks_l, vs_l, us_l = [], [], [], [], []
    for l in range(depth):
        shift, scale, gate = mod[l, :, :d], mod[l, :, d:2 * d], mod[l, :, 2 * d:]

        hp = _norm_mod(xp, norm_g, l, scale[:batch], shift[:batch], tm_norm, seq, BF16)
        act, side = _inproj(hp, w_in_b, sgu_ln_g, sgu_ln_b, l, tm_in, False)
        kp, vp = side[:, :COL], side[:, COL:]
        att = _moba_prompt(act, batch, seq)
        sgu = _sgu_prompt(act, w_s, b_s, l, tm_sgu)
        xp = _outproj(att, sgu, act, xp, gate[:batch], w_pa, w_pb, w_o, l, tm_out, seq)
        kp_l.append(kp.reshape(batch, seq, N_HEADS, HEAD_DIM))
        vp_l.append(vp.reshape(batch, seq, N_HEADS, HEAD_DIM))

        hs = _norm_mod(xs, norm_g, l, scale[batch:], shift[batch:], tm_s, 1, F32)
        act_s, side_s = _inproj(hs, w_in, sgu_ln_g, sgu_ln_b, l, tm_s, True)
        ks, vs, us = side_s[:, :COL], side_s[:, COL:2 * COL], side_s[:, 2 * COL:]
        heads = lambda a: a.reshape(dec_batch, N_HEADS, 1, HEAD_DIM)
        q_s = act_s[:, C_Q * COL:(C_Q + 1) * COL]
        za_s = act_s[:, C_ZA * COL:(C_ZA + 1) * COL]
        sel = _select(q_s.reshape(dec_batch, N_HEADS, HEAD_DIM), kmean, l)
        att_s = _moba_sample(sel, page_table, slopes, heads(q_s), heads(ks), heads(vs), heads(za_s),
                             cache_k, cache_v, l)
        w0 = jnp.repeat(w_s[l, :, 0, 0], GROUP_DIM).reshape(1, D_SGU)
        b0 = jnp.repeat(b_s[l, :, 0], GROUP_DIM).reshape(1, D_SGU)
        sgu_s = _sgu_sample(act_s, us, w0, b0)
        xs = _outproj(att_s.reshape(dec_batch, D_ATT), sgu_s, act_s, xs, gate[batch:], w_pa, w_pb, w_o, l, tm_s, 1)
        ks_l.append(ks.reshape(dec_batch, 1, N_HEADS, HEAD_DIM))
        vs_l.append(vs.reshape(dec_batch, 1, N_HEADS, HEAD_DIM))
        us_l.append(us.reshape(dec_batch, 1, D_SGU))

    y_prompt = _final_norm(xp, final_g, tm_norm).reshape(batch, seq, d)
    y_sample = _final_norm(xs, final_g, dec_batch).reshape(dec_batch, 1, d)
    return (y_prompt, y_sample, jnp.stack(kp_l), jnp.stack(vp_l), jnp.stack(ks_l), jnp.stack(vs_l), jnp.stack(us_l))
```

```python
import functools

import numpy as np
import jax
import jax.numpy as jnp
from jax import lax
from jax.experimental import pallas as pl
from jax.experimental.pallas import tpu as pltpu

D_MODEL = 2048
HEAD_DIM = 128
D_ATT = D_MODEL // 2
N_HEADS = D_ATT // HEAD_DIM
GROUP_DIM = 128
D_SGU = D_MODEL // 2
N_GROUPS = D_SGU // GROUP_DIM
MOBA_BLOCK = 256
TOPK = 3
CHUNK = 128
EPS = 1e-6
SCALE = HEAD_DIM ** -0.5
N_IN = 4 * D_ATT + 3 * D_SGU + 2 * D_MODEL

COL = 1024
N_COLS = N_IN // COL
C_Q, C_K, C_V, C_ZA, C_U, C_VB, C_ZB, C_GA, C_GB = 0, 1, 2, 3, 4, 5, 6, 7, 9

VMEM_LIMIT = 56 * 1024 * 1024
F32 = jnp.float32
BF16 = jnp.bfloat16
NEG_INF = float("-inf")


def _params(*sem):
    return pltpu.CompilerParams(dimension_semantics=sem, vmem_limit_bytes=VMEM_LIMIT)


def _dot(a, b):
    precision = lax.Precision.HIGHEST if a.dtype == F32 else None
    return jnp.dot(a, b, preferred_element_type=F32, precision=precision)


def _gelu(x):
    return 0.5 * x * (1.0 + lax.erf(x * (2.0 ** -0.5)))


def _dot_nt(a, b):
    return lax.dot_general(a, b, (((1,), (1,)), ((), ())), preferred_element_type=F32)


def _mod_kernel(c_ref, w_ref, b_ref, o_ref):
    c = c_ref[...]
    o_ref[...] = _dot(c * jax.nn.sigmoid(c), w_ref[...]) + b_ref[...]


def _mod(c_all, w_ada, b_ada):
    depth, d, n3 = w_ada.shape
    rows = c_all.shape[0]
    tn = 768
    return pl.pallas_call(
        _mod_kernel,
        grid=(depth, n3 // tn),
        in_specs=[
            pl.BlockSpec((rows, d), lambda l, n: (0, 0)),
            pl.BlockSpec((None, d, tn), lambda l, n: (l, 0, n)),
            pl.BlockSpec((None, 1, tn), lambda l, n: (l, 0, n)),
        ],
        out_specs=pl.BlockSpec((None, rows, tn), lambda l, n: (l, 0, n)),
        out_shape=jax.ShapeDtypeStruct((depth, rows, n3), F32),
        compiler_params=_params("arbitrary", "arbitrary"),
        name="adaln_mod",
    )(c_all, w_ada, b_ada.reshape(depth, 1, n3))


def _norm_kernel(x_ref, g_ref, scale_ref, shift_ref, o_ref):
    x = x_ref[...]
    y = x * lax.rsqrt(jnp.mean(x * x, axis=-1, keepdims=True) + EPS) * g_ref[...]
    o_ref[...] = (y * (1.0 + scale_ref[...]) + shift_ref[...]).astype(o_ref.dtype)


def _final_norm_kernel(x_ref, g_ref, o_ref):
    x = x_ref[...]
    o_ref[...] = x * lax.rsqrt(jnp.mean(x * x, axis=-1, keepdims=True) + EPS) * g_ref[...]


def _row_vec_spec(vec, tm, rows_per_batch):
    d = vec.shape[-1]
    if rows_per_batch == 1:
        return vec, pl.BlockSpec((tm, d), lambda m: (m, 0))
    assert rows_per_batch % tm == 0
    per = rows_per_batch // tm
    return vec.reshape(vec.shape[0], 1, d), pl.BlockSpec((None, 1, d), lambda m: (m // per, 0, 0))


def _norm_mod(x, g_all, layer, scale, shift, tm, rows_per_batch, out_dtype):
    rows, d = x.shape
    scale, scale_spec = _row_vec_spec(scale, tm, rows_per_batch)
    shift, shift_spec = _row_vec_spec(shift, tm, rows_per_batch)
    return pl.pallas_call(
        _norm_kernel,
        grid=(rows // tm,),
        in_specs=[
            pl.BlockSpec((tm, d), lambda m: (m, 0)),
            pl.BlockSpec((None, 1, d), lambda m: (layer, 0, 0)),
            scale_spec,
            shift_spec,
        ],
        out_specs=pl.BlockSpec((tm, d), lambda m: (m, 0)),
        out_shape=jax.ShapeDtypeStruct((rows, d), out_dtype),
        compiler_params=_params("arbitrary"),
        name="norm_mod",
    )(x, g_all.reshape(-1, 1, d), scale, shift)


def _final_norm(x, g, tm):
    rows, d = x.shape
    return pl.pallas_call(
        _final_norm_kernel,
        grid=(rows // tm,),
        in_specs=[pl.BlockSpec((tm, d), lambda m: (m, 0)), pl.BlockSpec((1, d), lambda m: (0, 0))],
        out_specs=pl.BlockSpec((tm, d), lambda m: (m, 0)),
        out_shape=jax.ShapeDtypeStruct((rows, d), F32),
        compiler_params=_params("arbitrary"),
        name="final_norm",
    )(x, g.reshape(1, d))


INPROJ_CHUNK = 256
STREAM_PAGES = 12


def _inproj_kernel(*refs, emit_vn, n_stream, pages_per_seq):
    if n_stream:
        pt_ref, h_ref, w_ref, lng_ref, lnb_ref = refs[:5]
        pages = refs[5:5 + n_stream]
        act_ref, side_ref, kmean_ref = refs[5 + n_stream:]
    else:
        h_ref, w_ref, lng_ref, lnb_ref, act_ref, side_ref = refs
    n = pl.program_id(1)
    tm = h_ref.shape[0]
    chunk = min(INPROJ_CHUNK, tm)

    if n_stream:
        ppb = MOBA_BLOCK // pages[0].shape[0]
        total = kmean_ref.shape[0] * pages_per_seq
        step = pl.program_id(0) * pl.num_programs(1) + n
        for i in range(n_stream // ppb):
            first = step * n_stream + i * ppb

            @pl.when(first < total)
            def _(i=i, first=first):
                tot = pages[i * ppb][...].sum(axis=0)
                for j in range(1, ppb):
                    tot = tot + pages[i * ppb + j][...].sum(axis=0)
                kmean_ref[first // pages_per_seq, (first % pages_per_seq) // ppb] = tot * (1.0 / MOBA_BLOCK)

    def column_block(epilogue):
        for c in range(tm // chunk):
            rows = pl.ds(c * chunk, chunk)
            acc = _dot(h_ref[rows, :], w_ref[...])
            act_ref[rows, :] = epilogue(rows, acc).astype(act_ref.dtype)

    def to_side(rows, acc):
        side_ref[rows, :] = acc
        return acc

    def norm_gelu(rows, acc):
        a = _gelu(acc)
        mu = jnp.mean(a, axis=-1, keepdims=True)
        c = a - mu
        var = jnp.mean(c * c, axis=-1, keepdims=True)
        vn = c * lax.rsqrt(var + EPS) * lng_ref[...] + lnb_ref[...]
        return to_side(rows, vn) if emit_vn else vn

    @pl.when(n >= C_GA)
    def _():
        column_block(lambda rows, acc: jax.nn.sigmoid(acc))

    @pl.when(n == C_Q)
    def _():
        column_block(lambda rows, acc: acc * SCALE)

    @pl.when((n == C_K) | (n == C_V))
    def _():
        column_block(to_side)

    @pl.when((n == C_ZA) | (n == C_ZB))
    def _():
        column_block(lambda rows, acc: jax.nn.silu(acc))

    @pl.when(n == C_U)
    def _():
        column_block(lambda rows, acc: _gelu(acc))

    @pl.when(n == C_VB)
    def _():
        column_block(norm_gelu)


def _inproj(h, w_all, ln_g, ln_b, layer, tm, emit_vn, cache_k=None, page_table=None):
    rows, d = h.shape
    assert h.dtype == w_all.dtype
    n_side = 3 if emit_vn else 2
    steps = (rows // tm) * N_COLS

    def side_index(m, n, *_):
        blk = jnp.where(n >= C_V, 1, 0)
        return (m, blk + jnp.where(n >= C_VB, 1, 0) if emit_vn else blk)

    vec = pl.BlockSpec((None, 1, COL), lambda m, n, *_: (layer, 0, 0))
    in_specs = [
        pl.BlockSpec((tm, d), lambda m, n, *_: (m, 0)),
        pl.BlockSpec((None, d, COL), lambda m, n, *_: (layer, 0, n)),
        vec,
        vec,
    ]
    out_specs = [pl.BlockSpec((tm, COL), lambda m, n, *_: (m, n)), pl.BlockSpec((tm, COL), side_index)]
    out_shape = [jax.ShapeDtypeStruct((rows, N_IN), h.dtype), jax.ShapeDtypeStruct((rows, n_side * COL), F32)]
    args = [h, w_all, ln_g.reshape(-1, 1, COL), ln_b.reshape(-1, 1, COL)]
    n_stream, pages_per_seq = 0, 0
    if cache_k is not None:
        _, _, page, nh, hd = cache_k.shape
        b, pages_per_seq = page_table.shape
        ppb = MOBA_BLOCK // page
        total = b * pages_per_seq
        n_stream = STREAM_PAGES
        assert n_stream % ppb == 0 and pages_per_seq % ppb == 0 and steps * n_stream >= total

        def page_spec(j):
            def index(m, n, pt):
                slot = jnp.minimum((m * N_COLS + n) * n_stream + j, total - 1)
                return (layer, pt[slot], 0, 0, 0)
            return pl.BlockSpec((None, None, page, nh, hd), index)

        in_specs += [page_spec(j) for j in range(n_stream)]
        out_specs.append(pl.BlockSpec((b, pages_per_seq // ppb, nh, hd), lambda m, n, pt: (0, 0, 0, 0)))
        out_shape.append(jax.ShapeDtypeStruct((b, pages_per_seq // ppb, nh, hd), F32))
        args = [page_table.reshape(-1)] + args + [cache_k] * n_stream
    return pl.pallas_call(
        functools.partial(_inproj_kernel, emit_vn=emit_vn, n_stream=n_stream, pages_per_seq=pages_per_seq),
        grid_spec=pltpu.PrefetchScalarGridSpec(
            num_scalar_prefetch=1 if n_stream else 0,
            grid=(rows // tm, N_COLS),
            in_specs=in_specs,
            out_specs=out_specs,
        ),
        out_shape=out_shape,
        compiler_params=_params("arbitrary", "arbitrary"),
        name="in_proj",
    )(*args)


SEL_W = 8
MASKED = -1e30


def _slope_parts():
    slopes = (2.0 ** (-8.0 * np.arange(1, N_HEADS + 1, dtype=np.float32) / N_HEADS)).astype(np.float32)
    parts, rest = [], slopes
    while rest.any():
        part = rest.astype(BF16).astype(np.float32)
        parts.append(part)
        rest = rest - part
    return slopes, np.stack(parts)


def _feature_tables(seq):
    _, parts = _slope_parts()
    n_parts = parts.shape[0]
    width = SEL_W + 4 * n_parts
    q_rows = -(-width // 16) * 16 - SEL_W
    assert seq // MOBA_BLOCK <= SEL_W and width <= HEAD_DIM
    pos = jnp.arange(seq, dtype=jnp.int32)
    blk_of = pos // MOBA_BLOCK
    a = (blk_of * MOBA_BLOCK).astype(F32)
    r = (pos % MOBA_BLOCK).astype(F32)
    k_cols = [(blk_of == m).astype(F32) for m in range(SEL_W)]
    zeros = jnp.zeros((seq,), F32)
    kfeat, qfeat = [], []
    for h in range(N_HEADS):
        kc, qr = list(k_cols), []
        for p in range(n_parts):
            s = float(parts[p, h])
            kc += [zeros - s, zeros - s, a, r]
            qr += [a, r, zeros + s, zeros + s]
        kc += [zeros] * (HEAD_DIM - len(kc))
        qr += [zeros] * (q_rows - len(qr))
        kfeat.append(jnp.stack(kc, axis=1))
        qfeat.append(jnp.stack(qr, axis=0))
    return jnp.stack(kfeat).astype(BF16), jnp.stack(qfeat)


def _moba_prompt_kernel(q_ref, k_ref, v_ref, za_ref, kfeat_ref, qfeat_ref, o_ref, qa_ref, ka_ref, va_ref, s_ref):
    blk = MOBA_BLOCK
    seq = k_ref.shape[0]
    nblk = seq // blk
    hd = HEAD_DIM

    @pl.when((pl.program_id(0) == 0) & (pl.program_id(1) == 0))
    def _():
        lane = lax.broadcasted_iota(jnp.int32, (seq, hd), 1)
        va_ref[:, hd:] = jnp.where(lane == 0, 1.0, 0.0).astype(BF16)

    q = q_ref[...]
    k = k_ref[...]
    qa_ref[:, :hd] = q
    ka_ref[:, :hd] = k
    ka_ref[:, hd:] = kfeat_ref[...]
    va_ref[:, :hd] = v_ref[...]

    km = jnp.sum(k.astype(F32).reshape(nblk, blk, hd), axis=1) * (1.0 / blk)
    if nblk < SEL_W:
        km = jnp.concatenate([km, jnp.zeros((SEL_W - nblk, hd), F32)], axis=0)
    km_hi = km.astype(BF16).astype(F32)
    gt = _dot_nt(jnp.concatenate([km_hi, km - km_hi], axis=0).astype(BF16), q)
    gate = gt[:SEL_W] + gt[SEL_W:]

    blk_row = lax.broadcasted_iota(jnp.int32, gate.shape, 0)
    blk_own = lax.broadcasted_iota(jnp.int32, gate.shape, 1) // blk
    eligible = blk_row < blk_own
    ge = jnp.where(eligible, gate, NEG_INF)
    rank = jnp.zeros(gate.shape, jnp.int32)
    for m in range(nblk):
        other = ge[m:m + 1, :]
        ahead = (other > ge) | ((other == ge) & (m < blk_row))
        rank = rank + ahead.astype(jnp.int32)
    allowed = (eligible & (rank < TOPK)) | (blk_row == blk_own)
    feat_t = jnp.concatenate([jnp.where(allowed, 0.0, MASKED), qfeat_ref[...]], axis=0).astype(BF16)

    eye = (lax.broadcasted_iota(jnp.int32, (blk, blk), 0) == lax.broadcasted_iota(jnp.int32, (blk, blk), 1))
    eye = jnp.where(eye, 1.0, 0.0).astype(BF16)
    pad = jnp.zeros((hd - feat_t.shape[0], blk), BF16)
    for t in range(nblk):
        rows = slice(t * blk, (t + 1) * blk)
        qa_ref[rows, hd:] = _dot_nt(eye, jnp.concatenate([feat_t[:, rows], pad], axis=0)).astype(BF16)

    causal = lax.broadcasted_iota(jnp.int32, (blk, blk), 0) >= lax.broadcasted_iota(jnp.int32, (blk, blk), 1)
    for t in range(nblk):
        rows = slice(t * blk, (t + 1) * blk)
        qt = qa_ref[rows, :]
        for n in range(t + 1):
            s = _dot_nt(qt, ka_ref[n * blk:(n + 1) * blk, :])
            if n == t:
                s = jnp.where(causal, s, MASKED)
            s_ref[n] = s
        m_el = s_ref[0]
        for n in range(1, t + 1):
            m_el = jnp.maximum(m_el, s_ref[n])
        m = jnp.max(m_el, axis=1, keepdims=True)
        acc = None
        for n in range(t + 1):
            p = jnp.exp(s_ref[n] - m).astype(BF16)
            pv = _dot(p, va_ref[n * blk:(n + 1) * blk, :])
            acc = pv if acc is None else acc + pv
        out = acc[:, :hd] / acc[:, hd:hd + 1] * za_ref[rows, :].astype(F32)
        o_ref[rows, :] = out.astype(o_ref.dtype)


def _moba_prompt(act, batch, seq):
    kfeat, qfeat = _feature_tables(seq)
    per = COL // HEAD_DIM
    head = lambda c0: pl.BlockSpec((seq, HEAD_DIM), lambda b, h: (b, c0 * per + h))
    return pl.pallas_call(
        _moba_prompt_kernel,
        grid=(batch, N_HEADS),
        in_specs=[
            head(C_Q),
            head(C_K),
            head(C_V),
            head(C_ZA),
            pl.BlockSpec((None, seq, HEAD_DIM), lambda b, h: (h, 0, 0)),
            pl.BlockSpec((None, qfeat.shape[1], seq), lambda b, h: (h, 0, 0)),
        ],
        out_specs=pl.BlockSpec((seq, HEAD_DIM), lambda b, h: (b, h)),
        out_shape=jax.ShapeDtypeStruct((batch * seq, D_ATT), BF16),
        scratch_shapes=[
            pltpu.VMEM((seq, 2 * HEAD_DIM), BF16),
            pltpu.VMEM((seq, 2 * HEAD_DIM), BF16),
            pltpu.VMEM((seq, 2 * HEAD_DIM), BF16),
            pltpu.VMEM((seq // MOBA_BLOCK, MOBA_BLOCK, MOBA_BLOCK), F32),
        ],
        compiler_params=_params("arbitrary", "arbitrary"),
        name="moba_prompt",
    )(act, act, act, act, kfeat, qfeat)


def _sgu_prompt_kernel(ws_ref, bs_ref, vn_ref, u_ref, zb_ref, o_ref):
    row = lax.broadcasted_iota(jnp.int32, (CHUNK, CHUNK), 0)
    col = lax.broadcasted_iota(jnp.int32, (CHUNK, CHUNK), 1)
    for g in range(N_GROUPS):
        ws = jnp.where(row >= col, ws_ref[g], 0.0).astype(BF16)
        bs = bs_ref[g]
        cols = slice(g * GROUP_DIM, (g + 1) * GROUP_DIM)
        for c in range(vn_ref.shape[0] // CHUNK):
            r = slice(c * CHUNK, (c + 1) * CHUNK)
            mix = _dot(ws, vn_ref[r, cols]) + bs
            o_ref[r, cols] = (u_ref[r, cols].astype(F32) * mix * zb_ref[r, cols].astype(F32)).astype(o_ref.dtype)


def _sgu_prompt(act, w_s, b_s, layer, tm):
    rows = act.shape[0]
    col = lambda c: pl.BlockSpec((tm, COL), lambda m: (m, c))
    return pl.pallas_call(
        _sgu_prompt_kernel,
        grid=(rows // tm,),
        in_specs=[
            pl.BlockSpec((None, N_GROUPS, CHUNK, CHUNK), lambda m: (layer, 0, 0, 0)),
            pl.BlockSpec((None, N_GROUPS, CHUNK, 1), lambda m: (layer, 0, 0, 0)),
            col(C_VB),
            col(C_U),
            col(C_ZB),
        ],
        out_specs=pl.BlockSpec((tm, D_SGU), lambda m: (m, 0)),
        out_shape=jax.ShapeDtypeStruct((rows, D_SGU), BF16),
        compiler_params=_params("arbitrary"),
        name="sgu_prompt",
    )(w_s, b_s.reshape(-1, N_GROUPS, CHUNK, 1), act, act, act)


def _outproj_kernel(a_ref, b_ref, ga0_ref, ga1_ref, gb0_ref, gb1_ref, x_ref, gate_ref, wpa_ref, wpb_ref, wout_ref,
                    g_ref, *rest, last):
    ya = _dot(a_ref[...], wpa_ref[...])
    yb = _dot(b_ref[...], wpb_ref[...])
    sga = jnp.concatenate([ga0_ref[...], ga1_ref[...]], axis=1).astype(F32)
    sgb = jnp.concatenate([gb0_ref[...], gb1_ref[...]], axis=1).astype(F32)
    mix = sga * ya + sgb * yb
    x = x_ref[...] + gate_ref[...] * _dot(mix.astype(BF16), wout_ref[...])
    xn = x * lax.rsqrt(jnp.mean(x * x, axis=-1, keepdims=True) + EPS) * g_ref[...]
    if last:
        (y_ref,) = rest
        y_ref[...] = xn
    else:
        scale_ref, shift_ref, x_out_ref, h_ref = rest
        x_out_ref[...] = x
        h_ref[...] = (xn * (1.0 + scale_ref[...]) + shift_ref[...]).astype(h_ref.dtype)


def _outproj(a, b, act, x, gate, w_pa, w_pb, w_out, layer, tm, rows_per_batch, norm_g, scale=None, shift=None):
    rows, d = x.shape
    last = scale is None
    gate, gate_spec = _row_vec_spec(gate, tm, rows_per_batch)
    resident = lambda w: pl.BlockSpec((None,) + w.shape[1:], lambda m: (layer, 0, 0), pipeline_mode=pl.Buffered(1))
    col = lambda c: pl.BlockSpec((tm, COL), lambda m: (m, c))
    tile = pl.BlockSpec((tm, d), lambda m: (m, 0))
    in_specs = [
        pl.BlockSpec((tm, D_ATT), lambda m: (m, 0)),
        pl.BlockSpec((tm, D_SGU), lambda m: (m, 0)),
        col(C_GA),
        col(C_GA + 1),
        col(C_GB),
        col(C_GB + 1),
        tile,
        gate_spec,
        resident(w_pa),
        resident(w_pb),
        resident(w_out),
        pl.BlockSpec((None, 1, d), lambda m: (0, 0, 0)),
    ]
    args = [a, b, act, act, act, act, x, gate, w_pa, w_pb, w_out, norm_g]
    if last:
        out_specs, out_shape = tile, jax.ShapeDtypeStruct((rows, d), F32)
    else:
        scale, scale_spec = _row_vec_spec(scale, tm, rows_per_batch)
        shift, shift_spec = _row_vec_spec(shift, tm, rows_per_batch)
        in_specs += [scale_spec, shift_spec]
        args += [scale, shift]
        out_specs = [tile, tile]
        out_shape = [jax.ShapeDtypeStruct((rows, d), F32), jax.ShapeDtypeStruct((rows, d), BF16)]
    return pl.pallas_call(
        functools.partial(_outproj_kernel, last=last),
        grid=(rows // tm,),
        in_specs=in_specs,
        out_specs=out_specs,
        out_shape=out_shape,
        compiler_params=_params("arbitrary"),
        name="out_proj",
    )(*args)


def _mix_sample_kernel(a_ref, b_ref, ga_ref, gb_ref, wpa_ref, wpb_ref, o_ref):
    o_ref[...] = ga_ref[...] * _dot(a_ref[...], wpa_ref[...]) + gb_ref[...] * _dot(b_ref[...], wpb_ref[...])


def _residual_sample_kernel(mix_ref, x_ref, gate_ref, wout_ref, o_ref):
    o_ref[...] = x_ref[...] + gate_ref[...] * _dot(mix_ref[...], wout_ref[...])


def _outproj_sample(a, b, act, x, gate, w_pa, w_pb, w_out, layer):
    rows, d = x.shape
    tn = 512
    per = COL // tn
    full = lambda arr: pl.BlockSpec(arr.shape, lambda n: (0, 0))
    cols = lambda c0: pl.BlockSpec((rows, tn), lambda n: (0, c0 * per + n))
    weight = lambda w: pl.BlockSpec((None, w.shape[1], tn), lambda n: (layer, 0, n))
    mix = pl.pallas_call(
        _mix_sample_kernel,
        grid=(d // tn,),
        in_specs=[full(a), full(b), cols(C_GA), cols(C_GB), weight(w_pa), weight(w_pb)],
        out_specs=cols(0),
        out_shape=jax.ShapeDtypeStruct((rows, d), F32),
        compiler_params=_params("arbitrary"),
        name="mix_sample",
    )(a, b, act, act, w_pa, w_pb)
    return pl.pallas_call(
        _residual_sample_kernel,
        grid=(d // tn,),
        in_specs=[full(mix), cols(0), cols(0), weight(w_out)],
        out_specs=cols(0),
        out_shape=jax.ShapeDtypeStruct((rows, d), F32),
        compiler_params=_params("arbitrary"),
        name="residual_sample",
    )(mix, x, gate, w_out)


def _select_kernel(q_ref, km_ref, sel_ref):
    q = q_ref[...].astype(F32)
    km = km_ref[...]
    g = jnp.sum(km * q[None], axis=-1, keepdims=True)
    nb = g.shape[0]
    idx = lax.broadcasted_iota(jnp.int32, g.shape, 0)
    for r in range(TOPK):
        best = jnp.max(g, axis=0, keepdims=True)
        pick = jnp.min(jnp.where(g == best, idx, nb), axis=0, keepdims=True)
        sel_ref[r:r + 1] = pick
        g = jnp.where(idx == pick, NEG_INF, g)


def _select(q, kmean):
    b, nh, hd = q.shape
    nb = kmean.shape[1]
    sel = pl.pallas_call(
        _select_kernel,
        grid=(b,),
        in_specs=[
            pl.BlockSpec((None, nh, hd), lambda i: (i, 0, 0)),
            pl.BlockSpec((None, nb, nh, hd), lambda i: (i, 0, 0, 0)),
        ],
        out_specs=pl.BlockSpec((None, TOPK, nh, 1), lambda i: (i, 0, 0, 0)),
        out_shape=jax.ShapeDtypeStruct((b, TOPK, nh, 1), jnp.int32),
        compiler_params=_params("arbitrary"),
        name="moba_select",
    )(q, kmean)
    return sel.reshape(b, TOPK, nh)


def _moba_sample_kernel(sel_ref, pt_ref, slopes_ref, q_ref, kn_ref, vn_ref, za_ref, ck_ref, cv_ref, o_ref,
                        kbuf, vbuf, sem, *, layer, past, page, n_pages):
    nh = q_ref.shape[0]
    ppb = MOBA_BLOCK // page
    ntile = TOPK * ppb
    b = pl.program_id(0)

    def block_of(h, t):
        return sel_ref[(b * TOPK + t // ppb) * nh + h]

    def tile_copies(h, t):
        pg = pt_ref[b * n_pages + block_of(h, t) * ppb + t % ppb]
        return (pltpu.make_async_copy(ck_ref.at[layer, pg, :, h, :], kbuf.at[h, t], sem.at[0, h, t]),
                pltpu.make_async_copy(cv_ref.at[layer, pg, :, h, :], vbuf.at[h, t], sem.at[1, h, t]))

    for h in range(nh):
        for t in range(ntile):
            for cp in tile_copies(h, t):
                cp.start()

    pos = lax.broadcasted_iota(jnp.int32, (page, 1), 0)
    for h in range(nh):
        slope = slopes_ref[h]
        q = q_ref[h]
        s_self = jnp.sum(kn_ref[h] * q, axis=1, keepdims=True)
        scores = []
        m = s_self
        for t in range(ntile):
            k_cp, v_cp = tile_copies(h, t)
            k_cp.wait()
            v_cp.wait()
            dist = (past - (block_of(h, t) * MOBA_BLOCK + (t % ppb) * page) - pos).astype(F32)
            s = jnp.sum(kbuf[h, t] * q, axis=1, keepdims=True) - slope * dist
            scores.append(s)
            m = jnp.maximum(m, jnp.max(s, axis=0, keepdims=True))
        p_self = jnp.exp(s_self - m)
        denom = p_self
        out = p_self * vn_ref[h]
        for t in range(ntile):
            p = jnp.exp(scores[t] - m)
            denom = denom + jnp.sum(p, axis=0, keepdims=True)
            out = out + jnp.sum(p * vbuf[h, t], axis=0, keepdims=True)
        o_ref[h] = out / denom * za_ref[h]


def _moba_sample(sel, page_table, slopes, q, k_new, v_new, za, cache_k, cache_v, layer):
    b, nh, _, hd = q.shape
    page = cache_k.shape[2]
    n_pages = page_table.shape[1]
    past = n_pages * page
    assert past % MOBA_BLOCK == 0 and past // MOBA_BLOCK >= TOPK
    ntile = TOPK * (MOBA_BLOCK // page)
    vec = pl.BlockSpec((None, nh, 1, hd), lambda bi, s, p: (bi, 0, 0, 0))
    hbm = pl.BlockSpec(memory_space=pl.ANY)
    return pl.pallas_call(
        functools.partial(_moba_sample_kernel, layer=layer, past=past, page=page, n_pages=n_pages),
        grid_spec=pltpu.PrefetchScalarGridSpec(
            num_scalar_prefetch=2,
            grid=(b,),
            in_specs=[pl.BlockSpec(memory_space=pltpu.SMEM), vec, vec, vec, vec, hbm, hbm],
            out_specs=vec,
            scratch_shapes=[
                pltpu.VMEM((nh, ntile, page, hd), F32),
                pltpu.VMEM((nh, ntile, page, hd), F32),
                pltpu.SemaphoreType.DMA((2, nh, ntile)),
            ],
        ),
        out_shape=jax.ShapeDtypeStruct((b, nh, 1, hd), F32),
        compiler_params=_params("arbitrary"),
        name="moba_sample",
    )(sel.reshape(-1), page_table.reshape(-1), slopes, q, k_new, v_new, za, cache_k, cache_v)


def _sgu_sample_kernel(u_ref, vn_ref, zb_ref, w0_ref, b0_ref, o_ref):
    mix = w0_ref[...] * vn_ref[...] + b0_ref[...]
    o_ref[...] = (u_ref[...].astype(F32) * mix * zb_ref[...].astype(F32)).astype(o_ref.dtype)


def _sgu_sample(act, vn, w0, b0):
    rows = act.shape[0]
    blk = lambda c: pl.BlockSpec((rows, COL), lambda i: (0, c))
    vec = pl.BlockSpec((1, COL), lambda i: (0, 0))
    return pl.pallas_call(
        _sgu_sample_kernel,
        grid=(1,),
        in_specs=[blk(C_U), blk(0), blk(C_ZB), vec, vec],
        out_specs=blk(0),
        out_shape=jax.ShapeDtypeStruct((rows, D_SGU), act.dtype),
        compiler_params=_params("arbitrary"),
        name="sgu_sample",
    )(act, vn, act, w0, b0)


def kernel(x_prompt, x_sample, cache_k, cache_v, page_table, c_prompt, c_sample, norm_g, w_ada, b_ada, w_in,
           sgu_ln_g, sgu_ln_b, w_s, b_s, w_proj_a, w_proj_b, w_out, final_g):
    batch, seq, d = x_prompt.shape
    dec_batch, dec_seq, _ = x_sample.shape
    assert dec_seq == 1
    depth = w_in.shape[0]
    slopes = jnp.asarray(_slope_parts()[0])

    mod = _mod(jnp.concatenate([c_prompt, c_sample], axis=0), w_ada, b_ada)
    shift, scale, gate = mod[:, :, :d], mod[:, :, d:2 * d], mod[:, :, 2 * d:]

    xp = x_prompt.reshape(batch * seq, d)
    xs = x_sample.reshape(dec_batch, d)
    tm_norm, tm_in, tm_sgu, tm_out, tm_s = 512, 1024, 1024, 256, dec_batch
    w_in_b, w_pa, w_pb, w_o = (w.astype(BF16) for w in (w_in, w_proj_a, w_proj_b, w_out))
    hp = _norm_mod(xp, norm_g, 0, scale[0, :batch], shift[0, :batch], tm_norm, seq, BF16)
    kp_l, vp_l, ks_l, vs_l, us_l = [], [], [], [], []
    for l in range(depth):
        act, side, kmean = _inproj(hp, w_in_b, sgu_ln_g, sgu_ln_b, l, tm_in, False, cache_k, page_table)
        kp, vp = side[:, :COL], side[:, COL:]
        att = _moba_prompt(act, batch, seq)
        sgu = _sgu_prompt(act, w_s, b_s, l, tm_sgu)
        if l + 1 < depth:
            xp, hp = _outproj(att, sgu, act, xp, gate[l, :batch], w_pa, w_pb, w_o, l, tm_out, seq,
                              norm_g[l + 1].reshape(1, 1, d), scale[l + 1, :batch], shift[l + 1, :batch])
        else:
            y_prompt = _outproj(att, sgu, act, xp, gate[l, :batch], w_pa, w_pb, w_o, l, tm_out, seq,
                                final_g.reshape(1, 1, d))
        kp_l.append(kp.reshape(batch, seq, N_HEADS, HEAD_DIM))
        vp_l.append(vp.reshape(batch, seq, N_HEADS, HEAD_DIM))

        hs = _norm_mod(xs, norm_g, l, scale[l, batch:], shift[l, batch:], tm_s, 1, F32)
        act_s, side_s = _inproj(hs, w_in, sgu_ln_g, sgu_ln_b, l, tm_s, True)
        ks, vs, us = side_s[:, :COL], side_s[:, COL:2 * COL], side_s[:, 2 * COL:]
        heads = lambda a: a.reshape(dec_batch, N_HEADS, 1, HEAD_DIM)
        q_s = act_s[:, C_Q * COL:(C_Q + 1) * COL]
        za_s = act_s[:, C_ZA * COL:(C_ZA + 1) * COL]
        sel = _select(q_s.reshape(dec_batch, N_HEADS, HEAD_DIM), kmean)
        att_s = _moba_sample(sel, page_table, slopes, heads(q_s), heads(ks), heads(vs), heads(za_s),
                             cache_k, cache_v, l)
        w0 = jnp.repeat(w_s[l, :, 0, 0], GROUP_DIM).reshape(1, D_SGU)
        b0 = jnp.repeat(b_s[l, :, 0], GROUP_DIM).reshape(1, D_SGU)
        sgu_s = _sgu_sample(act_s, us, w0, b0)
        xs = _outproj_sample(att_s.reshape(dec_batch, D_ATT), sgu_s, act_s, xs, gate[l, batch:],
                             w_proj_a, w_proj_b, w_out, l)
        ks_l.append(ks.reshape(dec_batch, 1, N_HEADS, HEAD_DIM))
        vs_l.append(vs.reshape(dec_batch, 1, N_HEADS, HEAD_DIM))
        us_l.append(us.reshape(dec_batch, 1, D_SGU))

    y_sample = _final_norm(xs, final_g, dec_batch).reshape(dec_batch, 1, d)
    return (y_prompt.reshape(batch, seq, d), y_sample, jnp.stack(kp_l), jnp.stack(vp_l), jnp.stack(ks_l),
            jnp.stack(vs_l), jnp.stack(us_l))
```

```python
import functools

import numpy as np
import jax
import jax.numpy as jnp
from jax import lax
from jax.experimental import pallas as pl
from jax.experimental.pallas import tpu as pltpu

D_MODEL = 2048
HEAD_DIM = 128
D_ATT = D_MODEL // 2
N_HEADS = D_ATT // HEAD_DIM
GROUP_DIM = 128
D_SGU = D_MODEL // 2
N_GROUPS = D_SGU // GROUP_DIM
MOBA_BLOCK = 256
TOPK = 3
CHUNK = 128
EPS = 1e-6
SCALE = HEAD_DIM ** -0.5
N_IN = 4 * D_ATT + 3 * D_SGU + 2 * D_MODEL

COL = 1024
N_COLS = N_IN // COL
C_Q, C_K, C_V, C_ZA, C_U, C_VB, C_ZB, C_GA, C_GB = 0, 1, 2, 3, 4, 5, 6, 7, 9

VMEM_LIMIT = 56 * 1024 * 1024
F32 = jnp.float32
BF16 = jnp.bfloat16
NEG_INF = float("-inf")


def _params(*sem):
    return pltpu.CompilerParams(dimension_semantics=sem, vmem_limit_bytes=VMEM_LIMIT)


def _dot(a, b):
    precision = lax.Precision.HIGHEST if a.dtype == F32 else None
    return jnp.dot(a, b, preferred_element_type=F32, precision=precision)


def _gelu(x):
    return 0.5 * x * (1.0 + lax.erf(x * (2.0 ** -0.5)))


def _dot_nt(a, b):
    return lax.dot_general(a, b, (((1,), (1,)), ((), ())), preferred_element_type=F32)


def _mod_kernel(c_ref, w_ref, b_ref, o_ref):
    c = c_ref[...]
    o_ref[...] = _dot(c * jax.nn.sigmoid(c), w_ref[...]) + b_ref[...]


def _mod(c_all, w_ada, b_ada):
    depth, d, n3 = w_ada.shape
    rows = c_all.shape[0]
    tn = 768
    return pl.pallas_call(
        _mod_kernel,
        grid=(depth, n3 // tn),
        in_specs=[
            pl.BlockSpec((rows, d), lambda l, n: (0, 0)),
            pl.BlockSpec((None, d, tn), lambda l, n: (l, 0, n)),
            pl.BlockSpec((None, 1, tn), lambda l, n: (l, 0, n)),
        ],
        out_specs=pl.BlockSpec((None, rows, tn), lambda l, n: (l, 0, n)),
        out_shape=jax.ShapeDtypeStruct((depth, rows, n3), F32),
        compiler_params=_params("arbitrary", "arbitrary"),
        name="adaln_mod",
    )(c_all, w_ada, b_ada.reshape(depth, 1, n3))


def _norm_kernel(x_ref, g_ref, scale_ref, shift_ref, o_ref):
    x = x_ref[...]
    y = x * lax.rsqrt(jnp.mean(x * x, axis=-1, keepdims=True) + EPS) * g_ref[...]
    o_ref[...] = (y * (1.0 + scale_ref[...]) + shift_ref[...]).astype(o_ref.dtype)


def _final_norm_kernel(x_ref, g_ref, o_ref):
    x = x_ref[...]
    o_ref[...] = x * lax.rsqrt(jnp.mean(x * x, axis=-1, keepdims=True) + EPS) * g_ref[...]


def _row_vec_spec(vec, tm, rows_per_batch):
    d = vec.shape[-1]
    if rows_per_batch == 1:
        return vec, pl.BlockSpec((tm, d), lambda m: (m, 0))
    assert rows_per_batch % tm == 0
    per = rows_per_batch // tm
    return vec.reshape(vec.shape[0], 1, d), pl.BlockSpec((None, 1, d), lambda m: (m // per, 0, 0))


def _norm_mod(x, g_all, layer, scale, shift, tm, rows_per_batch, out_dtype):
    rows, d = x.shape
    scale, scale_spec = _row_vec_spec(scale, tm, rows_per_batch)
    shift, shift_spec = _row_vec_spec(shift, tm, rows_per_batch)
    return pl.pallas_call(
        _norm_kernel,
        grid=(rows // tm,),
        in_specs=[
            pl.BlockSpec((tm, d), lambda m: (m, 0)),
            pl.BlockSpec((None, 1, d), lambda m: (layer, 0, 0)),
            scale_spec,
            shift_spec,
        ],
        out_specs=pl.BlockSpec((tm, d), lambda m: (m, 0)),
        out_shape=jax.ShapeDtypeStruct((rows, d), out_dtype),
        compiler_params=_params("arbitrary"),
        name="norm_mod",
    )(x, g_all.reshape(-1, 1, d), scale, shift)


def _final_norm(x, g, tm):
    rows, d = x.shape
    return pl.pallas_call(
        _final_norm_kernel,
        grid=(rows // tm,),
        in_specs=[pl.BlockSpec((tm, d), lambda m: (m, 0)), pl.BlockSpec((1, d), lambda m: (0, 0))],
        out_specs=pl.BlockSpec((tm, d), lambda m: (m, 0)),
        out_shape=jax.ShapeDtypeStruct((rows, d), F32),
        compiler_params=_params("arbitrary"),
        name="final_norm",
    )(x, g.reshape(1, d))


INPROJ_CHUNK = 256
IN_STREAM = 6
MOBA_STREAM = 16


def _stream_plan(cache_k, page_table, layer, first_slot, n_slots, n_stream, step_of):
    _, _, page, nh, hd = cache_k.shape
    ppb = MOBA_BLOCK // page
    assert n_stream % ppb == 0 and first_slot % ppb == 0 and n_slots % ppb == 0

    def page_spec(j):
        def index(*ids_and_table):
            *ids, table = ids_and_table
            rel = jnp.minimum(step_of(*ids) * n_stream + j, n_slots - ppb + j % ppb)
            return (layer, table[first_slot + rel], 0, 0, 0)
        return pl.BlockSpec((None, None, page, nh, hd), index)

    in_specs = [page_spec(j) for j in range(n_stream)]
    out_spec = pl.BlockSpec((n_slots // ppb, nh, hd), lambda *_: (0, 0, 0))
    return in_specs, out_spec, jax.ShapeDtypeStruct((n_slots // ppb, nh, hd), F32)


def _stream_block_means(pages, kmean_ref, step):
    ppb = MOBA_BLOCK // pages[0].shape[0]
    per_step = len(pages) // ppb
    ways = 8

    def page_sum(ref):
        p = ref[...]
        return p.reshape(ways, p.shape[0] // ways, *p.shape[1:]).sum(axis=1).sum(axis=0)

    for i in range(per_step):
        tot = page_sum(pages[i * ppb])
        for j in range(1, ppb):
            tot = tot + page_sum(pages[i * ppb + j])
        kmean_ref[jnp.minimum(step * per_step + i, kmean_ref.shape[0] - 1)] = tot * (1.0 / MOBA_BLOCK)


def _inproj_columns(n, h_ref, w_ref, lng_ref, lnb_ref, act_ref, keep_k, keep_v, keep_vn):
    tm = h_ref.shape[0]
    chunk = min(INPROJ_CHUNK, tm)

    def column_block(epilogue):
        for c in range(tm // chunk):
            rows = pl.ds(c * chunk, chunk)
            acc = _dot(h_ref[rows, :], w_ref[...])
            act_ref[rows, :] = epilogue(rows, acc).astype(act_ref.dtype)

    def kept(keep):
        def epilogue(rows, acc):
            keep(rows, acc)
            return acc
        return epilogue

    def norm_gelu(rows, acc):
        a = _gelu(acc)
        mu = jnp.mean(a, axis=-1, keepdims=True)
        c = a - mu
        var = jnp.mean(c * c, axis=-1, keepdims=True)
        vn = c * lax.rsqrt(var + EPS) * lng_ref[...] + lnb_ref[...]
        if keep_vn is not None:
            keep_vn(rows, vn)
        return vn

    @pl.when(n >= C_GA)
    def _():
        column_block(lambda rows, acc: jax.nn.sigmoid(acc))

    @pl.when(n == C_Q)
    def _():
        column_block(lambda rows, acc: acc * SCALE)

    @pl.when(n == C_K)
    def _():
        column_block(kept(keep_k))

    @pl.when(n == C_V)
    def _():
        column_block(kept(keep_v))

    @pl.when((n == C_ZA) | (n == C_ZB))
    def _():
        column_block(lambda rows, acc: jax.nn.silu(acc))

    @pl.when(n == C_U)
    def _():
        column_block(lambda rows, acc: _gelu(acc))

    @pl.when(n == C_VB)
    def _():
        column_block(norm_gelu)


def _inproj_kernel(h_ref, w_ref, lng_ref, lnb_ref, act_ref, side_ref):
    def keep(rows, value):
        side_ref[rows, :] = value

    _inproj_columns(pl.program_id(1), h_ref, w_ref, lng_ref, lnb_ref, act_ref, keep, keep, keep)


def _inproj_prompt_kernel(*refs, n_stream, n_prev):
    table_ref, h_ref, w_ref, lng_ref, lnb_ref = refs[:5]
    pages = refs[5:5 + n_stream]
    prev = refs[5 + n_stream:5 + n_stream + (2 if n_prev else 0)]
    act_ref, kmean_ref, k_out, v_out, kbuf, vbuf, sem, prev_sem = refs[5 + n_stream + len(prev):]
    m, n = pl.program_id(0), pl.program_id(1)
    tm = h_ref.shape[0]
    first_step = (m == 0) & (n == 0)
    last_step = (m == pl.num_programs(0) - 1) & (n == pl.num_programs(1) - 1)

    _stream_block_means(pages, kmean_ref, m * pl.num_programs(1) + n)

    def head_copies(buf, out, which, tile):
        return [pltpu.make_async_copy(buf.at[:, pl.ds(h * HEAD_DIM, HEAD_DIM)],
                                      out.at[n_prev, pl.ds(tile * tm, tm), h, :], sem.at[which, h])
                for h in range(N_HEADS)]

    def prev_copies():
        return [pltpu.make_async_copy(src, out.at[pl.ds(0, n_prev)], prev_sem.at[i])
                for i, (src, out) in enumerate(zip(prev, (k_out, v_out)))]

    if n_prev:
        @pl.when(first_step)
        def _():
            for cp in prev_copies():
                cp.start()

    def before(buf, out, which):
        @pl.when(m > 0)
        def _():
            for cp in head_copies(buf, out, which, m - 1):
                cp.wait()

    @pl.when(n == C_K)
    def _():
        before(kbuf, k_out, 0)

    @pl.when(n == C_V)
    def _():
        before(vbuf, v_out, 1)

    def keep_k(rows, value):
        kbuf[rows, :] = value

    def keep_v(rows, value):
        vbuf[rows, :] = value

    _inproj_columns(n, h_ref, w_ref, lng_ref, lnb_ref, act_ref, keep_k, keep_v, None)

    @pl.when(n == C_K)
    def _():
        for cp in head_copies(kbuf, k_out, 0, m):
            cp.start()

    @pl.when(n == C_V)
    def _():
        for cp in head_copies(vbuf, v_out, 1, m):
            cp.start()

    @pl.when(last_step)
    def _():
        for cp in head_copies(kbuf, k_out, 0, m) + head_copies(vbuf, v_out, 1, m) + (prev_copies() if n_prev else []):
            cp.wait()


def _inproj_specs(d, layer, tm):
    vec = pl.BlockSpec((None, 1, COL), lambda m, n, *_: (layer, 0, 0))
    in_specs = [
        pl.BlockSpec((tm, d), lambda m, n, *_: (m, 0)),
        pl.BlockSpec((None, d, COL), lambda m, n, *_: (layer, 0, n)),
        vec,
        vec,
    ]
    return in_specs, pl.BlockSpec((tm, COL), lambda m, n, *_: (m, n))


def _inproj(h, w_all, ln_g, ln_b, layer, tm):
    rows, d = h.shape
    assert h.dtype == w_all.dtype

    def side_index(m, n):
        return (m, jnp.where(n >= C_V, 1, 0) + jnp.where(n >= C_VB, 1, 0))

    in_specs, act_spec = _inproj_specs(d, layer, tm)
    return pl.pallas_call(
        _inproj_kernel,
        grid=(rows // tm, N_COLS),
        in_specs=in_specs,
        out_specs=[act_spec, pl.BlockSpec((tm, COL), side_index)],
        out_shape=[jax.ShapeDtypeStruct((rows, N_IN), h.dtype), jax.ShapeDtypeStruct((rows, 3 * COL), F32)],
        compiler_params=_params("arbitrary", "arbitrary"),
        name="in_proj",
    )(h, w_all, ln_g.reshape(-1, 1, COL), ln_b.reshape(-1, 1, COL))


def _inproj_prompt(h, w_all, ln_g, ln_b, layer, tm, cache_k, page_table, n_slots, n_stream, prev_k, prev_v):
    rows, d = h.shape
    n_prev = 0 if prev_k is None else prev_k.shape[0]
    assert n_prev == layer and (rows // tm) * N_COLS * n_stream >= n_slots
    in_specs, act_spec = _inproj_specs(d, layer, tm)
    page_specs, kmean_spec, kmean_shape = _stream_plan(cache_k, page_table, layer, 0, n_slots, n_stream,
                                                       lambda m, n: m * N_COLS + n)
    hbm = pl.BlockSpec(memory_space=pl.ANY)
    kv_shape = jax.ShapeDtypeStruct((n_prev + 1, rows, N_HEADS, HEAD_DIM), F32)
    prev = [prev_k, prev_v] if n_prev else []
    return pl.pallas_call(
        functools.partial(_inproj_prompt_kernel, n_stream=n_stream, n_prev=n_prev),
        grid_spec=pltpu.PrefetchScalarGridSpec(
            num_scalar_prefetch=1,
            grid=(rows // tm, N_COLS),
            in_specs=in_specs + page_specs + [hbm] * len(prev),
            out_specs=[act_spec, kmean_spec, hbm, hbm],
            scratch_shapes=[
                pltpu.VMEM((tm, COL), F32),
                pltpu.VMEM((tm, COL), F32),
                pltpu.SemaphoreType.DMA((2, N_HEADS)),
                pltpu.SemaphoreType.DMA((2,)),
            ],
        ),
        out_shape=[jax.ShapeDtypeStruct((rows, N_IN), BF16), kmean_shape, kv_shape, kv_shape],
        compiler_params=_params("arbitrary", "arbitrary"),
        name="in_proj_prompt",
    )(page_table.reshape(-1), h, w_all, ln_g.reshape(-1, 1, COL), ln_b.reshape(-1, 1, COL),
      *([cache_k] * n_stream), *prev)


SEL_W = 8
MASKED = -1e30


def _slope_parts():
    slopes = (2.0 ** (-8.0 * np.arange(1, N_HEADS + 1, dtype=np.float32) / N_HEADS)).astype(np.float32)
    parts, rest = [], slopes
    while rest.any():
        part = rest.astype(BF16).astype(np.float32)
        parts.append(part)
        rest = rest - part
    return slopes, np.stack(parts)


def _feature_tables(seq):
    _, parts = _slope_parts()
    n_parts = parts.shape[0]
    width = SEL_W + 4 * n_parts
    q_rows = -(-width // 16) * 16 - SEL_W
    assert seq // MOBA_BLOCK <= SEL_W and width <= HEAD_DIM
    pos = np.arange(seq)
    blk_of = pos // MOBA_BLOCK
    a = (blk_of * MOBA_BLOCK).astype(np.float32)
    r = (pos % MOBA_BLOCK).astype(np.float32)
    kfeat = np.zeros((N_HEADS, seq, HEAD_DIM), np.float32)
    qfeat = np.zeros((N_HEADS, q_rows, seq), np.float32)
    for m in range(SEL_W):
        kfeat[:, :, m] = blk_of == m
    for h in range(N_HEADS):
        for p in range(n_parts):
            c = SEL_W + 4 * p
            kfeat[h, :, c:c + 2] = -parts[p, h]
            kfeat[h, :, c + 2], kfeat[h, :, c + 3] = a, r
            qfeat[h, 4 * p], qfeat[h, 4 * p + 1] = a, r
            qfeat[h, 4 * p + 2:4 * p + 4] = parts[p, h]
    return jnp.asarray(kfeat.astype(BF16)), jnp.asarray(qfeat)


def _moba_prompt_kernel(*refs, n_stream):
    q_ref, k_ref, v_ref, za_ref, kfeat_ref, qfeat_ref = refs[1:7]
    pages = refs[7:7 + n_stream]
    o_ref, kmean_ref, qa_ref, ka_ref, va_ref, s_ref = refs[7 + n_stream:]
    blk = MOBA_BLOCK
    seq = k_ref.shape[0]
    nblk = seq // blk
    hd = HEAD_DIM
    _stream_block_means(pages, kmean_ref, pl.program_id(0) * pl.num_programs(1) + pl.program_id(1))

    @pl.when((pl.program_id(0) == 0) & (pl.program_id(1) == 0))
    def _():
        lane = lax.broadcasted_iota(jnp.int32, (seq, hd), 1)
        va_ref[:, hd:] = jnp.where(lane == 0, 1.0, 0.0).astype(BF16)

    q = q_ref[...]
    k = k_ref[...]
    qa_ref[:, :hd] = q
    ka_ref[:, :hd] = k
    ka_ref[:, hd:] = kfeat_ref[...]
    va_ref[:, :hd] = v_ref[...]

    km = jnp.sum(k.astype(F32).reshape(nblk, blk, hd), axis=1) * (1.0 / blk)
    if nblk < SEL_W:
        km = jnp.concatenate([km, jnp.zeros((SEL_W - nblk, hd), F32)], axis=0)
    km_hi = km.astype(BF16).astype(F32)
    gt = _dot_nt(jnp.concatenate([km_hi, km - km_hi], axis=0).astype(BF16), q)
    gate = gt[:SEL_W] + gt[SEL_W:]

    blk_row = lax.broadcasted_iota(jnp.int32, gate.shape, 0)
    blk_own = lax.broadcasted_iota(jnp.int32, gate.shape, 1) // blk
    eligible = blk_row < blk_own
    ge = jnp.where(eligible, gate, NEG_INF)
    rank = jnp.zeros(gate.shape, jnp.int32)
    for m in range(nblk):
        other = ge[m:m + 1, :]
        ahead = (other > ge) | ((other == ge) & (m < blk_row))
        rank = rank + ahead.astype(jnp.int32)
    allowed = (eligible & (rank < TOPK)) | (blk_row == blk_own)
    feat_t = jnp.concatenate([jnp.where(allowed, 0.0, MASKED), qfeat_ref[...]], axis=0).astype(BF16)

    eye = (lax.broadcasted_iota(jnp.int32, (blk, blk), 0) == lax.broadcasted_iota(jnp.int32, (blk, blk), 1))
    eye = jnp.where(eye, 1.0, 0.0).astype(BF16)
    pad = jnp.zeros((hd - feat_t.shape[0], blk), BF16)
    for t in range(nblk):
        rows = slice(t * blk, (t + 1) * blk)
        qa_ref[rows, hd:] = _dot_nt(eye, jnp.concatenate([feat_t[:, rows], pad], axis=0)).astype(BF16)

    causal = lax.broadcasted_iota(jnp.int32, (blk, blk), 0) >= lax.broadcasted_iota(jnp.int32, (blk, blk), 1)
    for t in range(nblk):
        rows = slice(t * blk, (t + 1) * blk)
        qt = qa_ref[rows, :]
        for n in range(t + 1):
            s = _dot_nt(qt, ka_ref[n * blk:(n + 1) * blk, :])
            if n == t:
                s = jnp.where(causal, s, MASKED)
            s_ref[n] = s
        m_el = s_ref[0]
        for n in range(1, t + 1):
            m_el = jnp.maximum(m_el, s_ref[n])
        m = jnp.max(m_el, axis=1, keepdims=True)
        acc = None
        for n in range(t + 1):
            p = jnp.exp(s_ref[n] - m).astype(BF16)
            pv = _dot(p, va_ref[n * blk:(n + 1) * blk, :])
            acc = pv if acc is None else acc + pv
        out = acc[:, :hd] / acc[:, hd:hd + 1] * za_ref[rows, :].astype(F32)
        o_ref[rows, :] = out.astype(o_ref.dtype)


def _moba_prompt(act, batch, seq, cache_k, page_table, layer, first_slot, n_slots, n_stream):
    assert batch * N_HEADS * n_stream >= n_slots
    kfeat, qfeat = _feature_tables(seq)
    per = COL // HEAD_DIM
    head = lambda c0: pl.BlockSpec((seq, HEAD_DIM), lambda b, h, *_: (b, c0 * per + h))
    page_specs, kmean_spec, kmean_shape = _stream_plan(cache_k, page_table, layer, first_slot, n_slots, n_stream,
                                                       lambda b, h: b * N_HEADS + h)
    return pl.pallas_call(
        functools.partial(_moba_prompt_kernel, n_stream=n_stream),
        grid_spec=pltpu.PrefetchScalarGridSpec(
            num_scalar_prefetch=1,
            grid=(batch, N_HEADS),
            in_specs=[
                head(C_Q),
                head(C_K),
                head(C_V),
                head(C_ZA),
                pl.BlockSpec((None, seq, HEAD_DIM), lambda b, h, *_: (h, 0, 0)),
                pl.BlockSpec((None, qfeat.shape[1], seq), lambda b, h, *_: (h, 0, 0)),
            ] + page_specs,
            out_specs=[pl.BlockSpec((seq, HEAD_DIM), lambda b, h, *_: (b, h)), kmean_spec],
            scratch_shapes=[
                pltpu.VMEM((seq, 2 * HEAD_DIM), BF16),
                pltpu.VMEM((seq, 2 * HEAD_DIM), BF16),
                pltpu.VMEM((seq, 2 * HEAD_DIM), BF16),
                pltpu.VMEM((seq // MOBA_BLOCK, MOBA_BLOCK, MOBA_BLOCK), F32),
            ],
        ),
        out_shape=[jax.ShapeDtypeStruct((batch * seq, D_ATT), BF16), kmean_shape],
        compiler_params=_params("arbitrary", "arbitrary"),
        name="moba_prompt",
    )(page_table.reshape(-1), act, act, act, act, kfeat, qfeat, *([cache_k] * n_stream))


def _sgu_prompt_kernel(ws_ref, bs_ref, vn_ref, u_ref, zb_ref, o_ref):
    row = lax.broadcasted_iota(jnp.int32, (CHUNK, CHUNK), 0)
    col = lax.broadcasted_iota(jnp.int32, (CHUNK, CHUNK), 1)
    for g in range(N_GROUPS):
        ws = jnp.where(row >= col, ws_ref[g], 0.0).astype(BF16)
        bs = bs_ref[g]
        cols = slice(g * GROUP_DIM, (g + 1) * GROUP_DIM)
        for c in range(vn_ref.shape[0] // CHUNK):
            r = slice(c * CHUNK, (c + 1) * CHUNK)
            mix = _dot(ws, vn_ref[r, cols]) + bs
            o_ref[r, cols] = (u_ref[r, cols].astype(F32) * mix * zb_ref[r, cols].astype(F32)).astype(o_ref.dtype)


def _sgu_prompt(act, w_s, b_s, layer, tm):
    rows = act.shape[0]
    col = lambda c: pl.BlockSpec((tm, COL), lambda m: (m, c))
    return pl.pallas_call(
        _sgu_prompt_kernel,
        grid=(rows // tm,),
        in_specs=[
            pl.BlockSpec((None, N_GROUPS, CHUNK, CHUNK), lambda m: (layer, 0, 0, 0)),
            pl.BlockSpec((None, N_GROUPS, CHUNK, 1), lambda m: (layer, 0, 0, 0)),
            col(C_VB),
            col(C_U),
            col(C_ZB),
        ],
        out_specs=pl.BlockSpec((tm, D_SGU), lambda m: (m, 0)),
        out_shape=jax.ShapeDtypeStruct((rows, D_SGU), BF16),
        compiler_params=_params("arbitrary"),
        name="sgu_prompt",
    )(w_s, b_s.reshape(-1, N_GROUPS, CHUNK, 1), act, act, act)


def _outproj_kernel(a_ref, b_ref, ga0_ref, ga1_ref, gb0_ref, gb1_ref, x_ref, gate_ref, wpa_ref, wpb_ref, wout_ref,
                    g_ref, *rest, last):
    ya = _dot(a_ref[...], wpa_ref[...])
    yb = _dot(b_ref[...], wpb_ref[...])
    sga = jnp.concatenate([ga0_ref[...], ga1_ref[...]], axis=1).astype(F32)
    sgb = jnp.concatenate([gb0_ref[...], gb1_ref[...]], axis=1).astype(F32)
    mix = sga * ya + sgb * yb
    x = x_ref[...] + gate_ref[...] * _dot(mix.astype(BF16), wout_ref[...])
    xn = x * lax.rsqrt(jnp.mean(x * x, axis=-1, keepdims=True) + EPS) * g_ref[...]
    if last:
        (y_ref,) = rest
        y_ref[...] = xn
    else:
        scale_ref, shift_ref, x_out_ref, h_ref = rest
        x_out_ref[...] = x
        h_ref[...] = (xn * (1.0 + scale_ref[...]) + shift_ref[...]).astype(h_ref.dtype)


def _outproj(a, b, act, x, gate, w_pa, w_pb, w_out, layer, tm, rows_per_batch, norm_g, scale=None, shift=None):
    rows, d = x.shape
    last = scale is None
    gate, gate_spec = _row_vec_spec(gate, tm, rows_per_batch)
    resident = lambda w: pl.BlockSpec((None,) + w.shape[1:], lambda m: (layer, 0, 0), pipeline_mode=pl.Buffered(1))
    col = lambda c: pl.BlockSpec((tm, COL), lambda m: (m, c))
    tile = pl.BlockSpec((tm, d), lambda m: (m, 0))
    in_specs = [
        pl.BlockSpec((tm, D_ATT), lambda m: (m, 0)),
        pl.BlockSpec((tm, D_SGU), lambda m: (m, 0)),
        col(C_GA),
        col(C_GA + 1),
        col(C_GB),
        col(C_GB + 1),
        tile,
        gate_spec,
        resident(w_pa),
        resident(w_pb),
        resident(w_out),
        pl.BlockSpec((None, 1, d), lambda m: (0, 0, 0)),
    ]
    args = [a, b, act, act, act, act, x, gate, w_pa, w_pb, w_out, norm_g]
    if last:
        out_specs, out_shape = tile, jax.ShapeDtypeStruct((rows, d), F32)
    else:
        scale, scale_spec = _row_vec_spec(scale, tm, rows_per_batch)
        shift, shift_spec = _row_vec_spec(shift, tm, rows_per_batch)
        in_specs += [scale_spec, shift_spec]
        args += [scale, shift]
        out_specs = [tile, tile]
        out_shape = [jax.ShapeDtypeStruct((rows, d), F32), jax.ShapeDtypeStruct((rows, d), BF16)]
    return pl.pallas_call(
        functools.partial(_outproj_kernel, last=last),
        grid=(rows // tm,),
        in_specs=in_specs,
        out_specs=out_specs,
        out_shape=out_shape,
        compiler_params=_params("arbitrary"),
        name="out_proj",
    )(*args)


def _mix_sample_kernel(a_ref, b_ref, ga_ref, gb_ref, wpa_ref, wpb_ref, o_ref):
    o_ref[...] = ga_ref[...] * _dot(a_ref[...], wpa_ref[...]) + gb_ref[...] * _dot(b_ref[...], wpb_ref[...])


def _residual_sample_kernel(mix_ref, x_ref, gate_ref, wout_ref, o_ref):
    o_ref[...] = x_ref[...] + gate_ref[...] * _dot(mix_ref[...], wout_ref[...])


def _outproj_sample(a, b, act, x, gate, w_pa, w_pb, w_out, layer):
    rows, d = x.shape
    tn = 512
    per = COL // tn
    full = lambda arr: pl.BlockSpec(arr.shape, lambda n: (0, 0))
    cols = lambda c0: pl.BlockSpec((rows, tn), lambda n: (0, c0 * per + n))
    weight = lambda w: pl.BlockSpec((None, w.shape[1], tn), lambda n: (layer, 0, n))
    mix = pl.pallas_call(
        _mix_sample_kernel,
        grid=(d // tn,),
        in_specs=[full(a), full(b), cols(C_GA), cols(C_GB), weight(w_pa), weight(w_pb)],
        out_specs=cols(0),
        out_shape=jax.ShapeDtypeStruct((rows, d), F32),
        compiler_params=_params("arbitrary"),
        name="mix_sample",
    )(a, b, act, act, w_pa, w_pb)
    return pl.pallas_call(
        _residual_sample_kernel,
        grid=(d // tn,),
        in_specs=[full(mix), cols(0), cols(0), weight(w_out)],
        out_specs=cols(0),
        out_shape=jax.ShapeDtypeStruct((rows, d), F32),
        compiler_params=_params("arbitrary"),
        name="residual_sample",
    )(mix, x, gate, w_out)


def _select_kernel(q_ref, km_ref, sel_ref):
    q = q_ref[...].astype(F32)
    km = km_ref[...]
    g = jnp.sum(km * q[None], axis=-1, keepdims=True)
    nb = g.shape[0]
    idx = lax.broadcasted_iota(jnp.int32, g.shape, 0)
    for r in range(TOPK):
        best = jnp.max(g, axis=0, keepdims=True)
        pick = jnp.min(jnp.where(g == best, idx, nb), axis=0, keepdims=True)
        sel_ref[r:r + 1] = pick
        g = jnp.where(idx == pick, NEG_INF, g)


def _select(q, kmean):
    b, nh, hd = q.shape
    nb = kmean.shape[1]
    sel = pl.pallas_call(
        _select_kernel,
        grid=(b,),
        in_specs=[
            pl.BlockSpec((None, nh, hd), lambda i: (i, 0, 0)),
            pl.BlockSpec((None, nb, nh, hd), lambda i: (i, 0, 0, 0)),
        ],
        out_specs=pl.BlockSpec((None, TOPK, nh, 1), lambda i: (i, 0, 0, 0)),
        out_shape=jax.ShapeDtypeStruct((b, TOPK, nh, 1), jnp.int32),
        compiler_params=_params("arbitrary"),
        name="moba_select",
    )(q, kmean)
    return sel.reshape(b, TOPK, nh)


def _moba_sample_kernel(sel_ref, pt_ref, slopes_ref, q_ref, kn_ref, vn_ref, za_ref, ck_ref, cv_ref, o_ref,
                        kbuf, vbuf, sem, *, layer, past, page, n_pages):
    nh = q_ref.shape[0]
    ppb = MOBA_BLOCK // page
    ntile = TOPK * ppb
    b = pl.program_id(0)

    def block_of(h, t):
        return sel_ref[(b * TOPK + t // ppb) * nh + h]

    def tile_copies(h, t):
        pg = pt_ref[b * n_pages + block_of(h, t) * ppb + t % ppb]
        return (pltpu.make_async_copy(ck_ref.at[layer, pg, :, h, :], kbuf.at[h, t], sem.at[0, h, t]),
                pltpu.make_async_copy(cv_ref.at[layer, pg, :, h, :], vbuf.at[h, t], sem.at[1, h, t]))

    for h in range(nh):
        for t in range(ntile):
            for cp in tile_copies(h, t):
                cp.start()

    pos = lax.broadcasted_iota(jnp.int32, (page, 1), 0)
    for h in range(nh):
        slope = slopes_ref[h]
        q = q_ref[h]
        s_self = jnp.sum(kn_ref[h] * q, axis=1, keepdims=True)
        scores = []
        m = s_self
        for t in range(ntile):
            k_cp, v_cp = tile_copies(h, t)
            k_cp.wait()
            v_cp.wait()
            dist = (past - (block_of(h, t) * MOBA_BLOCK + (t % ppb) * page) - pos).astype(F32)
            s = jnp.sum(kbuf[h, t] * q, axis=1, keepdims=True) - slope * dist
            scores.append(s)
            m = jnp.maximum(m, jnp.max(s, axis=0, keepdims=True))
        p_self = jnp.exp(s_self - m)
        denom = p_self
        out = p_self * vn_ref[h]
        for t in range(ntile):
            p = jnp.exp(scores[t] - m)
            denom = denom + jnp.sum(p, axis=0, keepdims=True)
            out = out + jnp.sum(p * vbuf[h, t], axis=0, keepdims=True)
        o_ref[h] = out / denom * za_ref[h]


def _moba_sample(sel, page_table, slopes, q, k_new, v_new, za, cache_k, cache_v, layer):
    b, nh, _, hd = q.shape
    page = cache_k.shape[2]
    n_pages = page_table.shape[1]
    past = n_pages * page
    assert past % MOBA_BLOCK == 0 and past // MOBA_BLOCK >= TOPK
    ntile = TOPK * (MOBA_BLOCK // page)
    vec = pl.BlockSpec((None, nh, 1, hd), lambda bi, s, p: (bi, 0, 0, 0))
    hbm = pl.BlockSpec(memory_space=pl.ANY)
    return pl.pallas_call(
        functools.partial(_moba_sample_kernel, layer=layer, past=past, page=page, n_pages=n_pages),
        grid_spec=pltpu.PrefetchScalarGridSpec(
            num_scalar_prefetch=2,
            grid=(b,),
            in_specs=[pl.BlockSpec(memory_space=pltpu.SMEM), vec, vec, vec, vec, hbm, hbm],
            out_specs=vec,
            scratch_shapes=[
                pltpu.VMEM((nh, ntile, page, hd), F32),
                pltpu.VMEM((nh, ntile, page, hd), F32),
                pltpu.SemaphoreType.DMA((2, nh, ntile)),
            ],
        ),
        out_shape=jax.ShapeDtypeStruct((b, nh, 1, hd), F32),
        compiler_params=_params("arbitrary"),
        name="moba_sample",
    )(sel.reshape(-1), page_table.reshape(-1), slopes, q, k_new, v_new, za, cache_k, cache_v)


def _sgu_sample_kernel(u_ref, vn_ref, zb_ref, w0_ref, b0_ref, o_ref):
    mix = w0_ref[...] * vn_ref[...] + b0_ref[...]
    o_ref[...] = (u_ref[...].astype(F32) * mix * zb_ref[...].astype(F32)).astype(o_ref.dtype)


def _sgu_sample(act, vn, w0, b0):
    rows = act.shape[0]
    blk = lambda c: pl.BlockSpec((rows, COL), lambda i: (0, c))
    vec = pl.BlockSpec((1, COL), lambda i: (0, 0))
    return pl.pallas_call(
        _sgu_sample_kernel,
        grid=(1,),
        in_specs=[blk(C_U), blk(0), blk(C_ZB), vec, vec],
        out_specs=blk(0),
        out_shape=jax.ShapeDtypeStruct((rows, D_SGU), act.dtype),
        compiler_params=_params("arbitrary"),
        name="sgu_sample",
    )(act, vn, act, w0, b0)


def kernel(x_prompt, x_sample, cache_k, cache_v, page_table, c_prompt, c_sample, norm_g, w_ada, b_ada, w_in,
           sgu_ln_g, sgu_ln_b, w_s, b_s, w_proj_a, w_proj_b, w_out, final_g):
    batch, seq, d = x_prompt.shape
    dec_batch, dec_seq, _ = x_sample.shape
    assert dec_seq == 1
    depth = w_in.shape[0]
    slopes = jnp.asarray(_slope_parts()[0])

    mod = _mod(jnp.concatenate([c_prompt, c_sample], axis=0), w_ada, b_ada)
    shift, scale, gate = mod[:, :, :d], mod[:, :, d:2 * d], mod[:, :, 2 * d:]

    xp = x_prompt.reshape(batch * seq, d)
    xs = x_sample.reshape(dec_batch, d)
    tm_norm, tm_in, tm_sgu, tm_out, tm_s = 512, 1024, 1024, 256, dec_batch
    w_in_b, w_pa, w_pb, w_o = (w.astype(BF16) for w in (w_in, w_proj_a, w_proj_b, w_out))
    hp = _norm_mod(xp, norm_g, 0, scale[0, :batch], shift[0, :batch], tm_norm, seq, BF16)

    n_pages = page_table.size
    ppb = MOBA_BLOCK // cache_k.shape[2]
    in_slots = min(n_pages, (batch * seq // tm_in) * N_COLS * IN_STREAM) // ppb * ppb
    n_blocks = page_table.shape[1] // ppb

    kp, vp = None, None
    ks_l, vs_l, us_l = [], [], []
    for l in range(depth):
        act, kmean_a, kp, vp = _inproj_prompt(hp, w_in_b, sgu_ln_g, sgu_ln_b, l, tm_in, cache_k, page_table,
                                              in_slots, IN_STREAM, kp, vp)
        att, kmean_b = _moba_prompt(act, batch, seq, cache_k, page_table, l, in_slots, n_pages - in_slots,
                                    MOBA_STREAM)
        kmean = jnp.concatenate([kmean_a, kmean_b], axis=0).reshape(dec_batch, n_blocks, N_HEADS, HEAD_DIM)
        sgu = _sgu_prompt(act, w_s, b_s, l, tm_sgu)
        if l + 1 < depth:
            xp, hp = _outproj(att, sgu, act, xp, gate[l, :batch], w_pa, w_pb, w_o, l, tm_out, seq,
                              norm_g[l + 1].reshape(1, 1, d), scale[l + 1, :batch], shift[l + 1, :batch])
        else:
            y_prompt = _outproj(att, sgu, act, xp, gate[l, :batch], w_pa, w_pb, w_o, l, tm_out, seq,
                                final_g.reshape(1, 1, d))

        hs = _norm_mod(xs, norm_g, l, scale[l, batch:], shift[l, batch:], tm_s, 1, F32)
        act_s, side_s = _inproj(hs, w_in, sgu_ln_g, sgu_ln_b, l, tm_s)
        ks, vs, us = side_s[:, :COL], side_s[:, COL:2 * COL], side_s[:, 2 * COL:]
        heads = lambda a: a.reshape(dec_batch, N_HEADS, 1, HEAD_DIM)
        q_s = act_s[:, C_Q * COL:(C_Q + 1) * COL]
        za_s = act_s[:, C_ZA * COL:(C_ZA + 1) * COL]
        sel = _select(q_s.reshape(dec_batch, N_HEADS, HEAD_DIM), kmean)
        att_s = _moba_sample(sel, page_table, slopes, heads(q_s), heads(ks), heads(vs), heads(za_s),
                             cache_k, cache_v, l)
        w0 = jnp.repeat(w_s[l, :, 0, 0], GROUP_DIM).reshape(1, D_SGU)
        b0 = jnp.repeat(b_s[l, :, 0], GROUP_DIM).reshape(1, D_SGU)
        sgu_s = _sgu_sample(act_s, us, w0, b0)
        xs = _outproj_sample(att_s.reshape(dec_batch, D_ATT), sgu_s, act_s, xs, gate[l, batch:],
                             w_proj_a, w_proj_b, w_out, l)
        ks_l.append(ks.reshape(dec_batch, 1, N_HEADS, HEAD_DIM))
        vs_l.append(vs.reshape(dec_batch, 1, N_HEADS, HEAD_DIM))
        us_l.append(us.reshape(dec_batch, 1, D_SGU))

    y_sample = _final_norm(xs, final_g, dec_batch).reshape(dec_batch, 1, d)
    prompt_kv = lambda a: a.reshape(depth, batch, seq, N_HEADS, HEAD_DIM)
    return (y_prompt.reshape(batch, seq, d), y_sample, prompt_kv(kp), prompt_kv(vp), jnp.stack(ks_l),
            jnp.stack(vs_l), jnp.stack(us_l))
```

```python
import functools

import numpy as np
import jax
import jax.numpy as jnp
from jax import lax
from jax.experimental import pallas as pl
from jax.experimental.pallas import tpu as pltpu

D_MODEL = 2048
HEAD_DIM = 128
D_ATT = D_MODEL // 2
N_HEADS = D_ATT // HEAD_DIM
GROUP_DIM = 128
D_SGU = D_MODEL // 2
N_GROUPS = D_SGU // GROUP_DIM
MOBA_BLOCK = 256
TOPK = 3
CHUNK = 128
EPS = 1e-6
SCALE = HEAD_DIM ** -0.5
N_IN = 4 * D_ATT + 3 * D_SGU + 2 * D_MODEL

COL = 1024
N_COLS = N_IN // COL
C_Q, C_K, C_V, C_ZA, C_U, C_VB, C_ZB, C_GA, C_GB = 0, 1, 2, 3, 4, 5, 6, 7, 9

VMEM_LIMIT = 56 * 1024 * 1024
F32 = jnp.float32
BF16 = jnp.bfloat16
NEG_INF = float("-inf")


def _params(*sem):
    return pltpu.CompilerParams(dimension_semantics=sem, vmem_limit_bytes=VMEM_LIMIT)


def _dot(a, b):
    precision = lax.Precision.HIGHEST if a.dtype == F32 else None
    return jnp.dot(a, b, preferred_element_type=F32, precision=precision)


def _gelu(x):
    return 0.5 * x * (1.0 + lax.erf(x * (2.0 ** -0.5)))


def _dot_nt(a, b):
    return lax.dot_general(a, b, (((1,), (1,)), ((), ())), preferred_element_type=F32)


def _mod_kernel(c_ref, w_ref, b_ref, o_ref):
    c = c_ref[...]
    o_ref[...] = _dot(c * jax.nn.sigmoid(c), w_ref[...]) + b_ref[...]


def _mod(c_all, w_ada, b_ada):
    depth, d, n3 = w_ada.shape
    rows = c_all.shape[0]
    tn = 768
    return pl.pallas_call(
        _mod_kernel,
        grid=(depth, n3 // tn),
        in_specs=[
            pl.BlockSpec((rows, d), lambda l, n: (0, 0)),
            pl.BlockSpec((None, d, tn), lambda l, n: (l, 0, n)),
            pl.BlockSpec((None, 1, tn), lambda l, n: (l, 0, n)),
        ],
        out_specs=pl.BlockSpec((None, rows, tn), lambda l, n: (l, 0, n)),
        out_shape=jax.ShapeDtypeStruct((depth, rows, n3), F32),
        compiler_params=_params("arbitrary", "arbitrary"),
        name="adaln_mod",
    )(c_all, w_ada, b_ada.reshape(depth, 1, n3))


def _norm_kernel(x_ref, g_ref, scale_ref, shift_ref, o_ref):
    x = x_ref[...]
    y = x * lax.rsqrt(jnp.mean(x * x, axis=-1, keepdims=True) + EPS) * g_ref[...]
    o_ref[...] = (y * (1.0 + scale_ref[...]) + shift_ref[...]).astype(o_ref.dtype)


def _final_norm_kernel(x_ref, g_ref, o_ref):
    x = x_ref[...]
    o_ref[...] = x * lax.rsqrt(jnp.mean(x * x, axis=-1, keepdims=True) + EPS) * g_ref[...]


def _row_vec_spec(vec, tm, rows_per_batch):
    d = vec.shape[-1]
    if rows_per_batch == 1:
        return vec, pl.BlockSpec((tm, d), lambda m: (m, 0))
    assert rows_per_batch % tm == 0
    per = rows_per_batch // tm
    return vec.reshape(vec.shape[0], 1, d), pl.BlockSpec((None, 1, d), lambda m: (m // per, 0, 0))


def _norm_mod(x, g_all, layer, scale, shift, tm, rows_per_batch, out_dtype):
    rows, d = x.shape
    scale, scale_spec = _row_vec_spec(scale, tm, rows_per_batch)
    shift, shift_spec = _row_vec_spec(shift, tm, rows_per_batch)
    return pl.pallas_call(
        _norm_kernel,
        grid=(rows // tm,),
        in_specs=[
            pl.BlockSpec((tm, d), lambda m: (m, 0)),
            pl.BlockSpec((None, 1, d), lambda m: (layer, 0, 0)),
            scale_spec,
            shift_spec,
        ],
        out_specs=pl.BlockSpec((tm, d), lambda m: (m, 0)),
        out_shape=jax.ShapeDtypeStruct((rows, d), out_dtype),
        compiler_params=_params("arbitrary"),
        name="norm_mod",
    )(x, g_all.reshape(-1, 1, d), scale, shift)


def _final_norm(x, g, tm):
    rows, d = x.shape
    return pl.pallas_call(
        _final_norm_kernel,
        grid=(rows // tm,),
        in_specs=[pl.BlockSpec((tm, d), lambda m: (m, 0)), pl.BlockSpec((1, d), lambda m: (0, 0))],
        out_specs=pl.BlockSpec((tm, d), lambda m: (m, 0)),
        out_shape=jax.ShapeDtypeStruct((rows, d), F32),
        compiler_params=_params("arbitrary"),
        name="final_norm",
    )(x, g.reshape(1, d))


INPROJ_CHUNK = 256
IN_STREAM = 6
MOBA_STREAM = 16


def _stream_plan(cache_k, page_table, layer, first_slot, n_slots, n_stream, step_of):
    _, _, page, nh, hd = cache_k.shape
    ppb = MOBA_BLOCK // page
    assert n_stream % ppb == 0 and first_slot % ppb == 0 and n_slots % ppb == 0

    def page_spec(j):
        def index(*ids_and_table):
            *ids, table = ids_and_table
            rel = jnp.minimum(step_of(*ids) * n_stream + j, n_slots - ppb + j % ppb)
            return (layer, table[first_slot + rel], 0, 0, 0)
        return pl.BlockSpec((None, None, page, nh, hd), index)

    in_specs = [page_spec(j) for j in range(n_stream)]
    out_spec = pl.BlockSpec((n_slots // ppb, nh, hd), lambda *_: (0, 0, 0))
    return in_specs, out_spec, jax.ShapeDtypeStruct((n_slots // ppb, nh, hd), F32)


def _stream_block_means(pages, kmean_ref, step):
    ppb = MOBA_BLOCK // pages[0].shape[0]
    per_step = len(pages) // ppb
    ways = 8

    def page_sum(ref):
        p = ref[...]
        return p.reshape(ways, p.shape[0] // ways, *p.shape[1:]).sum(axis=1).sum(axis=0)

    for i in range(per_step):
        tot = page_sum(pages[i * ppb])
        for j in range(1, ppb):
            tot = tot + page_sum(pages[i * ppb + j])
        kmean_ref[jnp.minimum(step * per_step + i, kmean_ref.shape[0] - 1)] = tot * (1.0 / MOBA_BLOCK)


def _inproj_columns(n, h_ref, w_ref, lng_ref, lnb_ref, act_ref, keep_k, keep_v, keep_vn):
    tm = h_ref.shape[0]
    chunk = min(INPROJ_CHUNK, tm)

    def column_block(epilogue):
        for c in range(tm // chunk):
            rows = pl.ds(c * chunk, chunk)
            acc = _dot(h_ref[rows, :], w_ref[...])
            act_ref[rows, :] = epilogue(rows, acc).astype(act_ref.dtype)

    def kept(keep):
        def epilogue(rows, acc):
            keep(rows, acc)
            return acc
        return epilogue

    def norm_gelu(rows, acc):
        a = _gelu(acc)
        mu = jnp.mean(a, axis=-1, keepdims=True)
        c = a - mu
        var = jnp.mean(c * c, axis=-1, keepdims=True)
        vn = c * lax.rsqrt(var + EPS) * lng_ref[...] + lnb_ref[...]
        if keep_vn is not None:
            keep_vn(rows, vn)
        return vn

    @pl.when(n >= C_GA)
    def _():
        column_block(lambda rows, acc: jax.nn.sigmoid(acc))

    @pl.when(n == C_Q)
    def _():
        column_block(lambda rows, acc: acc * SCALE)

    @pl.when(n == C_K)
    def _():
        column_block(kept(keep_k))

    @pl.when(n == C_V)
    def _():
        column_block(kept(keep_v))

    @pl.when((n == C_ZA) | (n == C_ZB))
    def _():
        column_block(lambda rows, acc: jax.nn.silu(acc))

    @pl.when(n == C_U)
    def _():
        column_block(lambda rows, acc: _gelu(acc))

    @pl.when(n == C_VB)
    def _():
        column_block(norm_gelu)


def _inproj_kernel(h_ref, w_ref, lng_ref, lnb_ref, act_ref, side_ref):
    def keep(rows, value):
        side_ref[rows, :] = value

    _inproj_columns(pl.program_id(1), h_ref, w_ref, lng_ref, lnb_ref, act_ref, keep, keep, keep)


def _inproj_prompt_kernel(*refs, n_stream, layer):
    table_ref, h_ref, w_ref, lng_ref, lnb_ref = refs[:5]
    pages = refs[5:5 + n_stream]
    act_ref, kmean_ref, k_out, v_out, kbuf, vbuf, sem = refs[5 + n_stream + 2:]
    m, n = pl.program_id(0), pl.program_id(1)
    tm = h_ref.shape[0]
    last_step = (m == pl.num_programs(0) - 1) & (n == pl.num_programs(1) - 1)

    _stream_block_means(pages, kmean_ref, m * pl.num_programs(1) + n)

    def head_copies(buf, out, which, tile):
        return [pltpu.make_async_copy(buf.at[:, pl.ds(h * HEAD_DIM, HEAD_DIM)],
                                      out.at[layer, pl.ds(tile * tm, tm), h, :], sem.at[which, h])
                for h in range(N_HEADS)]

    def before(buf, out, which):
        @pl.when(m > 0)
        def _():
            for cp in head_copies(buf, out, which, m - 1):
                cp.wait()

    @pl.when(n == C_K)
    def _():
        before(kbuf, k_out, 0)

    @pl.when(n == C_V)
    def _():
        before(vbuf, v_out, 1)

    def keep_k(rows, value):
        kbuf[rows, :] = value

    def keep_v(rows, value):
        vbuf[rows, :] = value

    _inproj_columns(n, h_ref, w_ref, lng_ref, lnb_ref, act_ref, keep_k, keep_v, None)

    @pl.when(n == C_K)
    def _():
        for cp in head_copies(kbuf, k_out, 0, m):
            cp.start()

    @pl.when(n == C_V)
    def _():
        for cp in head_copies(vbuf, v_out, 1, m):
            cp.start()

    @pl.when(last_step)
    def _():
        for cp in head_copies(kbuf, k_out, 0, m) + head_copies(vbuf, v_out, 1, m):
            cp.wait()


def _inproj_specs(d, layer, tm):
    vec = pl.BlockSpec((None, 1, COL), lambda m, n, *_: (layer, 0, 0))
    in_specs = [
        pl.BlockSpec((tm, d), lambda m, n, *_: (m, 0)),
        pl.BlockSpec((None, d, COL), lambda m, n, *_: (layer, 0, n)),
        vec,
        vec,
    ]
    return in_specs, pl.BlockSpec((tm, COL), lambda m, n, *_: (m, n))


def _inproj(h, w_all, ln_g, ln_b, layer, tm):
    rows, d = h.shape
    assert h.dtype == w_all.dtype

    def side_index(m, n):
        return (m, jnp.where(n >= C_V, 1, 0) + jnp.where(n >= C_VB, 1, 0))

    in_specs, act_spec = _inproj_specs(d, layer, tm)
    return pl.pallas_call(
        _inproj_kernel,
        grid=(rows // tm, N_COLS),
        in_specs=in_specs,
        out_specs=[act_spec, pl.BlockSpec((tm, COL), side_index)],
        out_shape=[jax.ShapeDtypeStruct((rows, N_IN), h.dtype), jax.ShapeDtypeStruct((rows, 3 * COL), F32)],
        compiler_params=_params("arbitrary", "arbitrary"),
        name="in_proj",
    )(h, w_all, ln_g.reshape(-1, 1, COL), ln_b.reshape(-1, 1, COL))


def _inproj_prompt(h, w_all, ln_g, ln_b, layer, tm, cache_k, page_table, n_slots, n_stream, k_all, v_all):
    rows, d = h.shape
    depth = w_all.shape[0]
    assert (rows // tm) * N_COLS * n_stream >= n_slots and k_all.shape == v_all.shape == (depth, rows, N_HEADS, HEAD_DIM)
    in_specs, act_spec = _inproj_specs(d, layer, tm)
    page_specs, kmean_spec, kmean_shape = _stream_plan(cache_k, page_table, layer, 0, n_slots, n_stream,
                                                       lambda m, n: m * N_COLS + n)
    hbm = pl.BlockSpec(memory_space=pl.ANY)
    kv_shape = jax.ShapeDtypeStruct((depth, rows, N_HEADS, HEAD_DIM), F32)
    first_stacked = 5 + n_stream
    return pl.pallas_call(
        functools.partial(_inproj_prompt_kernel, n_stream=n_stream, layer=layer),
        grid_spec=pltpu.PrefetchScalarGridSpec(
            num_scalar_prefetch=1,
            grid=(rows // tm, N_COLS),
            in_specs=in_specs + page_specs + [hbm, hbm],
            out_specs=[act_spec, kmean_spec, hbm, hbm],
            scratch_shapes=[
                pltpu.VMEM((tm, COL), F32),
                pltpu.VMEM((tm, COL), F32),
                pltpu.SemaphoreType.DMA((2, N_HEADS)),
            ],
        ),
        out_shape=[jax.ShapeDtypeStruct((rows, N_IN), BF16), kmean_shape, kv_shape, kv_shape],
        input_output_aliases={first_stacked: 2, first_stacked + 1: 3},
        compiler_params=_params("arbitrary", "arbitrary"),
        name="in_proj_prompt",
    )(page_table.reshape(-1), h, w_all, ln_g.reshape(-1, 1, COL), ln_b.reshape(-1, 1, COL),
      *([cache_k] * n_stream), k_all, v_all)


SEL_W = 8
MASKED = -1e30


def _slope_parts():
    slopes = (2.0 ** (-8.0 * np.arange(1, N_HEADS + 1, dtype=np.float32) / N_HEADS)).astype(np.float32)
    parts, rest = [], slopes
    while rest.any():
        part = rest.astype(BF16).astype(np.float32)
        parts.append(part)
        rest = rest - part
    return slopes, np.stack(parts)


def _feature_tables(seq):
    _, parts = _slope_parts()
    n_parts = parts.shape[0]
    width = SEL_W + 4 * n_parts
    q_rows = -(-width // 16) * 16 - SEL_W
    assert seq // MOBA_BLOCK <= SEL_W and width <= HEAD_DIM
    pos = np.arange(seq)
    blk_of = pos // MOBA_BLOCK
    a = (blk_of * MOBA_BLOCK).astype(np.float32)
    r = (pos % MOBA_BLOCK).astype(np.float32)
    kfeat = np.zeros((N_HEADS, seq, HEAD_DIM), np.float32)
    qfeat = np.zeros((N_HEADS, q_rows, seq), np.float32)
    for m in range(SEL_W):
        kfeat[:, :, m] = blk_of == m
    for h in range(N_HEADS):
        for p in range(n_parts):
            c = SEL_W + 4 * p
            kfeat[h, :, c:c + 2] = -parts[p, h]
            kfeat[h, :, c + 2], kfeat[h, :, c + 3] = a, r
            qfeat[h, 4 * p], qfeat[h, 4 * p + 1] = a, r
            qfeat[h, 4 * p + 2:4 * p + 4] = parts[p, h]
    return jnp.asarray(kfeat.astype(BF16)), jnp.asarray(qfeat)


def _moba_prompt_kernel(*refs, n_stream):
    q_ref, k_ref, v_ref, za_ref, kfeat_ref, qfeat_ref = refs[1:7]
    pages = refs[7:7 + n_stream]
    o_ref, kmean_ref, qa_ref, ka_ref, va_ref, s_ref = refs[7 + n_stream:]
    blk = MOBA_BLOCK
    seq = k_ref.shape[0]
    nblk = seq // blk
    hd = HEAD_DIM
    _stream_block_means(pages, kmean_ref, pl.program_id(0) * pl.num_programs(1) + pl.program_id(1))

    @pl.when((pl.program_id(0) == 0) & (pl.program_id(1) == 0))
    def _():
        lane = lax.broadcasted_iota(jnp.int32, (seq, hd), 1)
        va_ref[:, hd:] = jnp.where(lane == 0, 1.0, 0.0).astype(BF16)

    q = q_ref[...]
    k = k_ref[...]
    qa_ref[:, :hd] = q
    ka_ref[:, :hd] = k
    ka_ref[:, hd:] = kfeat_ref[...]
    va_ref[:, :hd] = v_ref[...]

    km = jnp.sum(k.astype(F32).reshape(nblk, blk, hd), axis=1) * (1.0 / blk)
    if nblk < SEL_W:
        km = jnp.concatenate([km, jnp.zeros((SEL_W - nblk, hd), F32)], axis=0)
    km_hi = km.astype(BF16).astype(F32)
    gt = _dot_nt(jnp.concatenate([km_hi, km - km_hi], axis=0).astype(BF16), q)
    gate = gt[:SEL_W] + gt[SEL_W:]

    blk_row = lax.broadcasted_iota(jnp.int32, gate.shape, 0)
    blk_own = lax.broadcasted_iota(jnp.int32, gate.shape, 1) // blk
    eligible = blk_row < blk_own
    ge = jnp.where(eligible, gate, NEG_INF)
    rank = jnp.zeros(gate.shape, jnp.int32)
    for m in range(nblk):
        other = ge[m:m + 1, :]
        ahead = (other > ge) | ((other == ge) & (m < blk_row))
        rank = rank + ahead.astype(jnp.int32)
    allowed = (eligible & (rank < TOPK)) | (blk_row == blk_own)
    feat_t = jnp.concatenate([jnp.where(allowed, 0.0, MASKED), qfeat_ref[...]], axis=0).astype(BF16)

    eye = (lax.broadcasted_iota(jnp.int32, (blk, blk), 0) == lax.broadcasted_iota(jnp.int32, (blk, blk), 1))
    eye = jnp.where(eye, 1.0, 0.0).astype(BF16)
    pad = jnp.zeros((hd - feat_t.shape[0], blk), BF16)
    for t in range(nblk):
        rows = slice(t * blk, (t + 1) * blk)
        qa_ref[rows, hd:] = _dot_nt(eye, jnp.concatenate([feat_t[:, rows], pad], axis=0)).astype(BF16)

    causal = lax.broadcasted_iota(jnp.int32, (blk, blk), 0) >= lax.broadcasted_iota(jnp.int32, (blk, blk), 1)
    for t in range(nblk):
        rows = slice(t * blk, (t + 1) * blk)
        qt = qa_ref[rows, :]
        for n in range(t + 1):
            s = _dot_nt(qt, ka_ref[n * blk:(n + 1) * blk, :])
            if n == t:
                s = jnp.where(causal, s, MASKED)
            s_ref[n] = s
        m_el = s_ref[0]
        for n in range(1, t + 1):
            m_el = jnp.maximum(m_el, s_ref[n])
        m = jnp.max(m_el, axis=1, keepdims=True)
        acc = None
        for n in range(t + 1):
            p = jnp.exp(s_ref[n] - m).astype(BF16)
            pv = _dot(p, va_ref[n * blk:(n + 1) * blk, :])
            acc = pv if acc is None else acc + pv
        out = acc[:, :hd] / acc[:, hd:hd + 1] * za_ref[rows, :].astype(F32)
        o_ref[rows, :] = out.astype(o_ref.dtype)


def _moba_prompt(act, batch, seq, cache_k, page_table, layer, first_slot, n_slots, n_stream):
    assert batch * N_HEADS * n_stream >= n_slots
    kfeat, qfeat = _feature_tables(seq)
    per = COL // HEAD_DIM
    head = lambda c0: pl.BlockSpec((seq, HEAD_DIM), lambda b, h, *_: (b, c0 * per + h))
    page_specs, kmean_spec, kmean_shape = _stream_plan(cache_k, page_table, layer, first_slot, n_slots, n_stream,
                                                       lambda b, h: b * N_HEADS + h)
    return pl.pallas_call(
        functools.partial(_moba_prompt_kernel, n_stream=n_stream),
        grid_spec=pltpu.PrefetchScalarGridSpec(
            num_scalar_prefetch=1,
            grid=(batch, N_HEADS),
            in_specs=[
                head(C_Q),
                head(C_K),
                head(C_V),
                head(C_ZA),
                pl.BlockSpec((None, seq, HEAD_DIM), lambda b, h, *_: (h, 0, 0)),
                pl.BlockSpec((None, qfeat.shape[1], seq), lambda b, h, *_: (h, 0, 0)),
            ] + page_specs,
            out_specs=[pl.BlockSpec((seq, HEAD_DIM), lambda b, h, *_: (b, h)), kmean_spec],
            scratch_shapes=[
                pltpu.VMEM((seq, 2 * HEAD_DIM), BF16),
                pltpu.VMEM((seq, 2 * HEAD_DIM), BF16),
                pltpu.VMEM((seq, 2 * HEAD_DIM), BF16),
                pltpu.VMEM((seq // MOBA_BLOCK, MOBA_BLOCK, MOBA_BLOCK), F32),
            ],
        ),
        out_shape=[jax.ShapeDtypeStruct((batch * seq, D_ATT), BF16), kmean_shape],
        compiler_params=_params("arbitrary", "arbitrary"),
        name="moba_prompt",
    )(page_table.reshape(-1), act, act, act, act, kfeat, qfeat, *([cache_k] * n_stream))


def _sgu_prompt_kernel(ws_ref, bs_ref, vn_ref, u_ref, zb_ref, o_ref):
    row = lax.broadcasted_iota(jnp.int32, (CHUNK, CHUNK), 0)
    col = lax.broadcasted_iota(jnp.int32, (CHUNK, CHUNK), 1)
    for g in range(N_GROUPS):
        ws = jnp.where(row >= col, ws_ref[g], 0.0).astype(BF16)
        bs = bs_ref[g]
        cols = slice(g * GROUP_DIM, (g + 1) * GROUP_DIM)
        for c in range(vn_ref.shape[0] // CHUNK):
            r = slice(c * CHUNK, (c + 1) * CHUNK)
            mix = _dot(ws, vn_ref[r, cols]) + bs
            o_ref[r, cols] = (u_ref[r, cols].astype(F32) * mix * zb_ref[r, cols].astype(F32)).astype(o_ref.dtype)


def _sgu_prompt(act, w_s, b_s, layer, tm):
    rows = act.shape[0]
    col = lambda c: pl.BlockSpec((tm, COL), lambda m: (m, c))
    return pl.pallas_call(
        _sgu_prompt_kernel,
        grid=(rows // tm,),
        in_specs=[
            pl.BlockSpec((None, N_GROUPS, CHUNK, CHUNK), lambda m: (layer, 0, 0, 0)),
            pl.BlockSpec((None, N_GROUPS, CHUNK, 1), lambda m: (layer, 0, 0, 0)),
            col(C_VB),
            col(C_U),
            col(C_ZB),
        ],
        out_specs=pl.BlockSpec((tm, D_SGU), lambda m: (m, 0)),
        out_shape=jax.ShapeDtypeStruct((rows, D_SGU), BF16),
        compiler_params=_params("arbitrary"),
        name="sgu_prompt",
    )(w_s, b_s.reshape(-1, N_GROUPS, CHUNK, 1), act, act, act)


def _outproj_kernel(a_ref, b_ref, ga0_ref, ga1_ref, gb0_ref, gb1_ref, x_ref, gate_ref, wpa_ref, wpb_ref, wout_ref,
                    g_ref, *rest, last):
    ya = _dot(a_ref[...], wpa_ref[...])
    yb = _dot(b_ref[...], wpb_ref[...])
    sga = jnp.concatenate([ga0_ref[...], ga1_ref[...]], axis=1).astype(F32)
    sgb = jnp.concatenate([gb0_ref[...], gb1_ref[...]], axis=1).astype(F32)
    mix = sga * ya + sgb * yb
    x = x_ref[...] + gate_ref[...] * _dot(mix.astype(BF16), wout_ref[...])
    xn = x * lax.rsqrt(jnp.mean(x * x, axis=-1, keepdims=True) + EPS) * g_ref[...]
    if last:
        (y_ref,) = rest
        y_ref[...] = xn
    else:
        scale_ref, shift_ref, x_out_ref, h_ref = rest
        x_out_ref[...] = x
        h_ref[...] = (xn * (1.0 + scale_ref[...]) + shift_ref[...]).astype(h_ref.dtype)


def _outproj(a, b, act, x, gate, w_pa, w_pb, w_out, layer, tm, rows_per_batch, norm_g, scale=None, shift=None):
    rows, d = x.shape
    last = scale is None
    gate, gate_spec = _row_vec_spec(gate, tm, rows_per_batch)
    resident = lambda w: pl.BlockSpec((None,) + w.shape[1:], lambda m: (layer, 0, 0), pipeline_mode=pl.Buffered(1))
    col = lambda c: pl.BlockSpec((tm, COL), lambda m: (m, c))
    tile = pl.BlockSpec((tm, d), lambda m: (m, 0))
    in_specs = [
        pl.BlockSpec((tm, D_ATT), lambda m: (m, 0)),
        pl.BlockSpec((tm, D_SGU), lambda m: (m, 0)),
        col(C_GA),
        col(C_GA + 1),
        col(C_GB),
        col(C_GB + 1),
        tile,
        gate_spec,
        resident(w_pa),
        resident(w_pb),
        resident(w_out),
        pl.BlockSpec((None, 1, d), lambda m: (0, 0, 0)),
    ]
    args = [a, b, act, act, act, act, x, gate, w_pa, w_pb, w_out, norm_g]
    if last:
        out_specs, out_shape = tile, jax.ShapeDtypeStruct((rows, d), F32)
    else:
        scale, scale_spec = _row_vec_spec(scale, tm, rows_per_batch)
        shift, shift_spec = _row_vec_spec(shift, tm, rows_per_batch)
        in_specs += [scale_spec, shift_spec]
        args += [scale, shift]
        out_specs = [tile, tile]
        out_shape = [jax.ShapeDtypeStruct((rows, d), F32), jax.ShapeDtypeStruct((rows, d), BF16)]
    return pl.pallas_call(
        functools.partial(_outproj_kernel, last=last),
        grid=(rows // tm,),
        in_specs=in_specs,
        out_specs=out_specs,
        out_shape=out_shape,
        compiler_params=_params("arbitrary"),
        name="out_proj",
    )(*args)


def _mix_sample_kernel(a_ref, b_ref, ga_ref, gb_ref, wpa_ref, wpb_ref, o_ref):
    o_ref[...] = ga_ref[...] * _dot(a_ref[...], wpa_ref[...]) + gb_ref[...] * _dot(b_ref[...], wpb_ref[...])


def _residual_sample_kernel(mix_ref, x_ref, gate_ref, wout_ref, o_ref):
    o_ref[...] = x_ref[...] + gate_ref[...] * _dot(mix_ref[...], wout_ref[...])


def _outproj_sample(a, b, act, x, gate, w_pa, w_pb, w_out, layer):
    rows, d = x.shape
    tn = 512
    per = COL // tn
    full = lambda arr: pl.BlockSpec(arr.shape, lambda n: (0, 0))
    cols = lambda c0: pl.BlockSpec((rows, tn), lambda n: (0, c0 * per + n))
    weight = lambda w: pl.BlockSpec((None, w.shape[1], tn), lambda n: (layer, 0, n))
    mix = pl.pallas_call(
        _mix_sample_kernel,
        grid=(d // tn,),
        in_specs=[full(a), full(b), cols(C_GA), cols(C_GB), weight(w_pa), weight(w_pb)],
        out_specs=cols(0),
        out_shape=jax.ShapeDtypeStruct((rows, d), F32),
        compiler_params=_params("arbitrary"),
        name="mix_sample",
    )(a, b, act, act, w_pa, w_pb)
    return pl.pallas_call(
        _residual_sample_kernel,
        grid=(d // tn,),
        in_specs=[full(mix), cols(0), cols(0), weight(w_out)],
        out_specs=cols(0),
        out_shape=jax.ShapeDtypeStruct((rows, d), F32),
        compiler_params=_params("arbitrary"),
        name="residual_sample",
    )(mix, x, gate, w_out)


def _select_kernel(q_ref, km_ref, sel_ref):
    q = q_ref[...].astype(F32)
    km = km_ref[...]
    g = jnp.sum(km * q[None], axis=-1, keepdims=True)
    nb = g.shape[0]
    idx = lax.broadcasted_iota(jnp.int32, g.shape, 0)
    for r in range(TOPK):
        best = jnp.max(g, axis=0, keepdims=True)
        pick = jnp.min(jnp.where(g == best, idx, nb), axis=0, keepdims=True)
        sel_ref[r:r + 1] = pick
        g = jnp.where(idx == pick, NEG_INF, g)


def _select(q, kmean):
    b, nh, hd = q.shape
    nb = kmean.shape[1]
    sel = pl.pallas_call(
        _select_kernel,
        grid=(b,),
        in_specs=[
            pl.BlockSpec((None, nh, hd), lambda i: (i, 0, 0)),
            pl.BlockSpec((None, nb, nh, hd), lambda i: (i, 0, 0, 0)),
        ],
        out_specs=pl.BlockSpec((None, TOPK, nh, 1), lambda i: (i, 0, 0, 0)),
        out_shape=jax.ShapeDtypeStruct((b, TOPK, nh, 1), jnp.int32),
        compiler_params=_params("arbitrary"),
        name="moba_select",
    )(q, kmean)
    return sel.reshape(b, TOPK, nh)


def _moba_sample_kernel(sel_ref, pt_ref, slopes_ref, q_ref, kn_ref, vn_ref, za_ref, ck_ref, cv_ref, o_ref,
                        kbuf, vbuf, sem, *, layer, past, page, n_pages):
    nh = q_ref.shape[0]
    ppb = MOBA_BLOCK // page
    ntile = TOPK * ppb
    b = pl.program_id(0)
    slot = b % 2

    def block_of(seq, h, t):
        return sel_ref[(seq * TOPK + t // ppb) * nh + h]

    def copies(seq):
        out = []
        for h in range(nh):
            for t in range(ntile):
                pg = pt_ref[seq * n_pages + block_of(seq, h, t) * ppb + t % ppb]
                dst = (seq % 2, h, t)
                out.append(pltpu.make_async_copy(ck_ref.at[layer, pg, :, h, :], kbuf.at[dst], sem.at[(0,) + dst]))
                out.append(pltpu.make_async_copy(cv_ref.at[layer, pg, :, h, :], vbuf.at[dst], sem.at[(1,) + dst]))
        return out

    @pl.when(b == 0)
    def _():
        for cp in copies(b):
            cp.start()

    @pl.when(b + 1 < pl.num_programs(0))
    def _():
        for cp in copies(b + 1):
            cp.start()

    for cp in copies(b):
        cp.wait()

    pos = lax.broadcasted_iota(jnp.int32, (page, 1), 0)
    for h in range(nh):
        slope = slopes_ref[h]
        q = q_ref[h]
        s_self = jnp.sum(kn_ref[h] * q, axis=1, keepdims=True)
        scores = []
        m = s_self
        for t in range(ntile):
            dist = (past - (block_of(b, h, t) * MOBA_BLOCK + (t % ppb) * page) - pos).astype(F32)
            s = jnp.sum(kbuf[slot, h, t] * q, axis=1, keepdims=True) - slope * dist
            scores.append(s)
            m = jnp.maximum(m, jnp.max(s, axis=0, keepdims=True))
        p_self = jnp.exp(s_self - m)
        denom = p_self
        out = p_self * vn_ref[h]
        for t in range(ntile):
            p = jnp.exp(scores[t] - m)
            denom = denom + jnp.sum(p, axis=0, keepdims=True)
            out = out + jnp.sum(p * vbuf[slot, h, t], axis=0, keepdims=True)
        o_ref[h] = out / denom * za_ref[h]


def _moba_sample(sel, page_table, slopes, q, k_new, v_new, za, cache_k, cache_v, layer):
    b, nh, _, hd = q.shape
    page = cache_k.shape[2]
    n_pages = page_table.shape[1]
    past = n_pages * page
    assert past % MOBA_BLOCK == 0 and past // MOBA_BLOCK >= TOPK
    ntile = TOPK * (MOBA_BLOCK // page)
    vec = pl.BlockSpec((None, nh, 1, hd), lambda bi, s, p: (bi, 0, 0, 0))
    hbm = pl.BlockSpec(memory_space=pl.ANY)
    return pl.pallas_call(
        functools.partial(_moba_sample_kernel, layer=layer, past=past, page=page, n_pages=n_pages),
        grid_spec=pltpu.PrefetchScalarGridSpec(
            num_scalar_prefetch=2,
            grid=(b,),
            in_specs=[pl.BlockSpec(memory_space=pltpu.SMEM), vec, vec, vec, vec, hbm, hbm],
            out_specs=vec,
            scratch_shapes=[
                pltpu.VMEM((2, nh, ntile, page, hd), F32),
                pltpu.VMEM((2, nh, ntile, page, hd), F32),
                pltpu.SemaphoreType.DMA((2, 2, nh, ntile)),
            ],
        ),
        out_shape=jax.ShapeDtypeStruct((b, nh, 1, hd), F32),
        compiler_params=_params("arbitrary"),
        name="moba_sample",
    )(sel.reshape(-1), page_table.reshape(-1), slopes, q, k_new, v_new, za, cache_k, cache_v)


def _sgu_sample_kernel(u_ref, vn_ref, zb_ref, w0_ref, b0_ref, o_ref):
    mix = w0_ref[...] * vn_ref[...] + b0_ref[...]
    o_ref[...] = (u_ref[...].astype(F32) * mix * zb_ref[...].astype(F32)).astype(o_ref.dtype)


def _sgu_sample(act, vn, w0, b0):
    rows = act.shape[0]
    blk = lambda c: pl.BlockSpec((rows, COL), lambda i: (0, c))
    vec = pl.BlockSpec((1, COL), lambda i: (0, 0))
    return pl.pallas_call(
        _sgu_sample_kernel,
        grid=(1,),
        in_specs=[blk(C_U), blk(0), blk(C_ZB), vec, vec],
        out_specs=blk(0),
        out_shape=jax.ShapeDtypeStruct((rows, D_SGU), act.dtype),
        compiler_params=_params("arbitrary"),
        name="sgu_sample",
    )(act, vn, act, w0, b0)


def kernel(x_prompt, x_sample, cache_k, cache_v, page_table, c_prompt, c_sample, norm_g, w_ada, b_ada, w_in,
           sgu_ln_g, sgu_ln_b, w_s, b_s, w_proj_a, w_proj_b, w_out, final_g):
    batch, seq, d = x_prompt.shape
    dec_batch, dec_seq, _ = x_sample.shape
    assert dec_seq == 1
    depth = w_in.shape[0]
    slopes = jnp.asarray(_slope_parts()[0])

    mod = _mod(jnp.concatenate([c_prompt, c_sample], axis=0), w_ada, b_ada)
    shift, scale, gate = mod[:, :, :d], mod[:, :, d:2 * d], mod[:, :, 2 * d:]

    xp = x_prompt.reshape(batch * seq, d)
    xs = x_sample.reshape(dec_batch, d)
    tm_norm, tm_in, tm_sgu, tm_out, tm_s = 512, 1024, 1024, 256, dec_batch
    w_in_b, w_pa, w_pb, w_o = (w.astype(BF16) for w in (w_in, w_proj_a, w_proj_b, w_out))
    hp = _norm_mod(xp, norm_g, 0, scale[0, :batch], shift[0, :batch], tm_norm, seq, BF16)

    n_pages = page_table.size
    ppb = MOBA_BLOCK // cache_k.shape[2]
    in_slots = min(n_pages, (batch * seq // tm_in) * N_COLS * IN_STREAM) // ppb * ppb
    n_blocks = page_table.shape[1] // ppb

    kp = jnp.zeros((depth, batch * seq, N_HEADS, HEAD_DIM), F32)
    vp = jnp.zeros((depth, batch * seq, N_HEADS, HEAD_DIM), F32)
    ks_l, vs_l, us_l = [], [], []
    for l in range(depth):
        act, kmean_a, kp, vp = _inproj_prompt(hp, w_in_b, sgu_ln_g, sgu_ln_b, l, tm_in, cache_k, page_table,
                                              in_slots, IN_STREAM, kp, vp)
        att, kmean_b = _moba_prompt(act, batch, seq, cache_k, page_table, l, in_slots, n_pages - in_slots,
                                    MOBA_STREAM)
        kmean = jnp.concatenate([kmean_a, kmean_b], axis=0).reshape(dec_batch, n_blocks, N_HEADS, HEAD_DIM)
        sgu = _sgu_prompt(act, w_s, b_s, l, tm_sgu)
        if l + 1 < depth:
            xp, hp = _outproj(att, sgu, act, xp, gate[l, :batch], w_pa, w_pb, w_o, l, tm_out, seq,
                              norm_g[l + 1].reshape(1, 1, d), scale[l + 1, :batch], shift[l + 1, :batch])
        else:
            y_prompt = _outproj(att, sgu, act, xp, gate[l, :batch], w_pa, w_pb, w_o, l, tm_out, seq,
                                final_g.reshape(1, 1, d))

        hs = _norm_mod(xs, norm_g, l, scale[l, batch:], shift[l, batch:], tm_s, 1, F32)
        act_s, side_s = _inproj(hs, w_in, sgu_ln_g, sgu_ln_b, l, tm_s)
        ks, vs, us = side_s[:, :COL], side_s[:, COL:2 * COL], side_s[:, 2 * COL:]
        heads = lambda a: a.reshape(dec_batch, N_HEADS, 1, HEAD_DIM)
        q_s = act_s[:, C_Q * COL:(C_Q + 1) * COL]
        za_s = act_s[:, C_ZA * COL:(C_ZA + 1) * COL]
        sel = _select(q_s.reshape(dec_batch, N_HEADS, HEAD_DIM), kmean)
        att_s = _moba_sample(sel, page_table, slopes, heads(q_s), heads(ks), heads(vs), heads(za_s),
                             cache_k, cache_v, l)
        w0 = jnp.repeat(w_s[l, :, 0, 0], GROUP_DIM).reshape(1, D_SGU)
        b0 = jnp.repeat(b_s[l, :, 0], GROUP_DIM).reshape(1, D_SGU)
        sgu_s = _sgu_sample(act_s, us, w0, b0)
        xs = _outproj_sample(att_s.reshape(dec_batch, D_ATT), sgu_s, act_s, xs, gate[l, batch:],
                             w_proj_a, w_proj_b, w_out, l)
        ks_l.append(ks.reshape(dec_batch, 1, N_HEADS, HEAD_DIM))
        vs_l.append(vs.reshape(dec_batch, 1, N_HEADS, HEAD_DIM))
        us_l.append(us.reshape(dec_batch, 1, D_SGU))

    y_sample = _final_norm(xs, final_g, dec_batch).reshape(dec_batch, 1, d)
    prompt_kv = lambda a: a.reshape(depth, batch, seq, N_HEADS, HEAD_DIM)
    return (y_prompt.reshape(batch, seq, d), y_sample, prompt_kv(kp), prompt_kv(vp), jnp.stack(ks_l),
            jnp.stack(vs_l), jnp.stack(us_l))
```

```python
import functools

import numpy as np
import jax
import jax.numpy as jnp
from jax import lax
from jax.experimental import pallas as pl
from jax.experimental.pallas import tpu as pltpu

D_MODEL = 2048
HEAD_DIM = 128
D_ATT = D_MODEL // 2
N_HEADS = D_ATT // HEAD_DIM
GROUP_DIM = 128
D_SGU = D_MODEL // 2
N_GROUPS = D_SGU // GROUP_DIM
MOBA_BLOCK = 256
TOPK = 3
CHUNK = 128
EPS = 1e-6
SCALE = HEAD_DIM ** -0.5
N_IN = 4 * D_ATT + 3 * D_SGU + 2 * D_MODEL

COL = 1024
N_COLS = N_IN // COL
C_Q, C_K, C_V, C_ZA, C_U, C_VB, C_ZB, C_GA, C_GB = 0, 1, 2, 3, 4, 5, 6, 7, 9

VMEM_LIMIT = 56 * 1024 * 1024
F32 = jnp.float32
BF16 = jnp.bfloat16
NEG_INF = float("-inf")


def _params(*sem):
    return pltpu.CompilerParams(dimension_semantics=sem, vmem_limit_bytes=VMEM_LIMIT)


def _dot(a, b):
    precision = lax.Precision.HIGHEST if a.dtype == F32 else None
    return jnp.dot(a, b, preferred_element_type=F32, precision=precision)


def _gelu(x):
    return 0.5 * x * (1.0 + lax.erf(x * (2.0 ** -0.5)))


def _dot_nt(a, b):
    return lax.dot_general(a, b, (((1,), (1,)), ((), ())), preferred_element_type=F32)


def _mod_kernel(c_ref, w_ref, b_ref, o_ref):
    c = c_ref[...]
    o_ref[...] = _dot(c * jax.nn.sigmoid(c), w_ref[...]) + b_ref[...]


def _mod(c_all, w_ada, b_ada):
    depth, d, n3 = w_ada.shape
    rows = c_all.shape[0]
    tn = 768
    return pl.pallas_call(
        _mod_kernel,
        grid=(depth, n3 // tn),
        in_specs=[
            pl.BlockSpec((rows, d), lambda l, n: (0, 0)),
            pl.BlockSpec((None, d, tn), lambda l, n: (l, 0, n)),
            pl.BlockSpec((None, 1, tn), lambda l, n: (l, 0, n)),
        ],
        out_specs=pl.BlockSpec((None, rows, tn), lambda l, n: (l, 0, n)),
        out_shape=jax.ShapeDtypeStruct((depth, rows, n3), F32),
        compiler_params=_params("arbitrary", "arbitrary"),
        name="adaln_mod",
    )(c_all, w_ada, b_ada.reshape(depth, 1, n3))


def _norm_kernel(x_ref, g_ref, scale_ref, shift_ref, o_ref):
    x = x_ref[...]
    y = x * lax.rsqrt(jnp.mean(x * x, axis=-1, keepdims=True) + EPS) * g_ref[...]
    o_ref[...] = (y * (1.0 + scale_ref[...]) + shift_ref[...]).astype(o_ref.dtype)


def _final_norm_kernel(x_ref, g_ref, o_ref):
    x = x_ref[...]
    o_ref[...] = x * lax.rsqrt(jnp.mean(x * x, axis=-1, keepdims=True) + EPS) * g_ref[...]


def _row_vec_spec(vec, tm, rows_per_batch):
    d = vec.shape[-1]
    if rows_per_batch == 1:
        return vec, pl.BlockSpec((tm, d), lambda m: (m, 0))
    assert rows_per_batch % tm == 0
    per = rows_per_batch // tm
    return vec.reshape(vec.shape[0], 1, d), pl.BlockSpec((None, 1, d), lambda m: (m // per, 0, 0))


def _norm_mod(x, g_all, layer, scale, shift, tm, rows_per_batch, out_dtype):
    rows, d = x.shape
    scale, scale_spec = _row_vec_spec(scale, tm, rows_per_batch)
    shift, shift_spec = _row_vec_spec(shift, tm, rows_per_batch)
    return pl.pallas_call(
        _norm_kernel,
        grid=(rows // tm,),
        in_specs=[
            pl.BlockSpec((tm, d), lambda m: (m, 0)),
            pl.BlockSpec((None, 1, d), lambda m: (layer, 0, 0)),
            scale_spec,
            shift_spec,
        ],
        out_specs=pl.BlockSpec((tm, d), lambda m: (m, 0)),
        out_shape=jax.ShapeDtypeStruct((rows, d), out_dtype),
        compiler_params=_params("arbitrary"),
        name="norm_mod",
    )(x, g_all.reshape(-1, 1, d), scale, shift)


def _final_norm(x, g, tm):
    rows, d = x.shape
    return pl.pallas_call(
        _final_norm_kernel,
        grid=(rows // tm,),
        in_specs=[pl.BlockSpec((tm, d), lambda m: (m, 0)), pl.BlockSpec((1, d), lambda m: (0, 0))],
        out_specs=pl.BlockSpec((tm, d), lambda m: (m, 0)),
        out_shape=jax.ShapeDtypeStruct((rows, d), F32),
        compiler_params=_params("arbitrary"),
        name="final_norm",
    )(x, g.reshape(1, d))


INPROJ_CHUNK = 256
IN_STREAM = 10
MOBA_STREAM = 6


def _stream_plan(cache_k, page_table, layer, first_slot, n_slots, n_stream, step_of):
    _, _, page, nh, hd = cache_k.shape
    ppb = MOBA_BLOCK // page
    assert n_stream % ppb == 0 and first_slot % ppb == 0 and n_slots % ppb == 0

    def page_spec(j):
        def index(*ids_and_table):
            *ids, table = ids_and_table
            rel = jnp.minimum(step_of(*ids) * n_stream + j, n_slots - ppb + j % ppb)
            return (layer, table[first_slot + rel], 0, 0, 0)
        return pl.BlockSpec((None, None, page, nh, hd), index)

    in_specs = [page_spec(j) for j in range(n_stream)]
    out_spec = pl.BlockSpec((n_slots // ppb, nh, hd), lambda *_: (0, 0, 0))
    return in_specs, out_spec, jax.ShapeDtypeStruct((n_slots // ppb, nh, hd), F32)


def _stream_block_means(pages, kmean_ref, step):
    ppb = MOBA_BLOCK // pages[0].shape[0]
    per_step = len(pages) // ppb
    ways = 8

    def page_sum(ref):
        p = ref[...]
        return p.reshape(ways, p.shape[0] // ways, *p.shape[1:]).sum(axis=1).sum(axis=0)

    for i in range(per_step):
        tot = page_sum(pages[i * ppb])
        for j in range(1, ppb):
            tot = tot + page_sum(pages[i * ppb + j])
        kmean_ref[jnp.minimum(step * per_step + i, kmean_ref.shape[0] - 1)] = tot * (1.0 / MOBA_BLOCK)


def _inproj_columns(n, h_ref, w_ref, lng_ref, lnb_ref, act_ref, keep_k, keep_v, keep_vn, rider=None):
    tm = h_ref.shape[0]
    chunk = min(INPROJ_CHUNK, tm)

    def column_block(epilogue):
        if rider is not None:
            rider()
        for c in range(tm // chunk):
            rows = pl.ds(c * chunk, chunk)
            acc = _dot(h_ref[rows, :], w_ref[...])
            act_ref[rows, :] = epilogue(rows, acc).astype(act_ref.dtype)

    def kept(keep):
        def epilogue(rows, acc):
            keep(rows, acc)
            return acc
        return epilogue

    def norm_gelu(rows, acc):
        a = _gelu(acc)
        mu = jnp.mean(a, axis=-1, keepdims=True)
        c = a - mu
        var = jnp.mean(c * c, axis=-1, keepdims=True)
        vn = c * lax.rsqrt(var + EPS) * lng_ref[...] + lnb_ref[...]
        if keep_vn is not None:
            keep_vn(rows, vn)
        return vn

    @pl.when(n >= C_GA)
    def _():
        column_block(lambda rows, acc: jax.nn.sigmoid(acc))

    @pl.when(n == C_Q)
    def _():
        column_block(lambda rows, acc: acc * SCALE)

    @pl.when(n == C_K)
    def _():
        column_block(kept(keep_k))

    @pl.when(n == C_V)
    def _():
        column_block(kept(keep_v))

    @pl.when((n == C_ZA) | (n == C_ZB))
    def _():
        column_block(lambda rows, acc: jax.nn.silu(acc))

    @pl.when(n == C_U)
    def _():
        column_block(lambda rows, acc: _gelu(acc))

    @pl.when(n == C_VB)
    def _():
        column_block(norm_gelu)


def _inproj_kernel(h_ref, w_ref, lng_ref, lnb_ref, act_ref, side_ref):
    def keep(rows, value):
        side_ref[rows, :] = value

    _inproj_columns(pl.program_id(1), h_ref, w_ref, lng_ref, lnb_ref, act_ref, keep, keep, keep)


def _inproj_prompt_kernel(*refs, n_stream, layer, aliased):
    table_ref, h_ref, w_ref, lng_ref, lnb_ref = refs[:5]
    pages = refs[5:5 + n_stream]
    act_ref, kmean_ref, k_out, v_out, kbuf, vbuf, sem = refs[5 + n_stream + (2 if aliased else 0):]
    m, n = pl.program_id(0), pl.program_id(1)
    tm = h_ref.shape[0]
    last_step = (m == pl.num_programs(0) - 1) & (n == pl.num_programs(1) - 1)
    slots = (layer,) if aliased else tuple(range(layer, k_out.shape[0]))

    def head_copies(buf, out, which, tile):
        return [pltpu.make_async_copy(buf.at[:, pl.ds(h * HEAD_DIM, HEAD_DIM)],
                                      out.at[slot, pl.ds(tile * tm, tm), h, :], sem.at[which, i, h])
                for i, slot in enumerate(slots) for h in range(N_HEADS)]

    def before(buf, out, which):
        @pl.when(m > 0)
        def _():
            for cp in head_copies(buf, out, which, m - 1):
                cp.wait()

    @pl.when(n == C_K)
    def _():
        before(kbuf, k_out, 0)

    @pl.when(n == C_V)
    def _():
        before(vbuf, v_out, 1)

    def keep_k(rows, value):
        kbuf[rows, :] = value

    def keep_v(rows, value):
        vbuf[rows, :] = value

    _inproj_columns(n, h_ref, w_ref, lng_ref, lnb_ref, act_ref, keep_k, keep_v, None,
                    rider=lambda: _stream_block_means(pages, kmean_ref, m * pl.num_programs(1) + n))

    @pl.when(n == C_K)
    def _():
        for cp in head_copies(kbuf, k_out, 0, m):
            cp.start()

    @pl.when(n == C_V)
    def _():
        for cp in head_copies(vbuf, v_out, 1, m):
            cp.start()

    @pl.when(last_step)
    def _():
        for cp in head_copies(kbuf, k_out, 0, m) + head_copies(vbuf, v_out, 1, m):
            cp.wait()


def _inproj_specs(d, layer, tm):
    vec = pl.BlockSpec((None, 1, COL), lambda m, n, *_: (layer, 0, 0))
    in_specs = [
        pl.BlockSpec((tm, d), lambda m, n, *_: (m, 0)),
        pl.BlockSpec((None, d, COL), lambda m, n, *_: (layer, 0, n)),
        vec,
        vec,
    ]
    return in_specs, pl.BlockSpec((tm, COL), lambda m, n, *_: (m, n))


def _inproj(h, w_all, ln_g, ln_b, layer, tm):
    rows, d = h.shape
    assert h.dtype == w_all.dtype

    def side_index(m, n):
        return (m, jnp.where(n >= C_V, 1, 0) + jnp.where(n >= C_VB, 1, 0))

    in_specs, act_spec = _inproj_specs(d, layer, tm)
    return pl.pallas_call(
        _inproj_kernel,
        grid=(rows // tm, N_COLS),
        in_specs=in_specs,
        out_specs=[act_spec, pl.BlockSpec((tm, COL), side_index)],
        out_shape=[jax.ShapeDtypeStruct((rows, N_IN), h.dtype), jax.ShapeDtypeStruct((rows, 3 * COL), F32)],
        compiler_params=_params("arbitrary", "arbitrary"),
        name="in_proj",
    )(h, w_all, ln_g.reshape(-1, 1, COL), ln_b.reshape(-1, 1, COL))


def _inproj_prompt(h, w_all, ln_g, ln_b, layer, tm, cache_k, page_table, n_slots, n_stream, k_all, v_all):
    rows, d = h.shape
    depth = w_all.shape[0]
    aliased = k_all is not None
    assert aliased == (layer > 0) and (rows // tm) * N_COLS * n_stream >= n_slots
    in_specs, act_spec = _inproj_specs(d, layer, tm)
    page_specs, kmean_spec, kmean_shape = _stream_plan(cache_k, page_table, layer, 0, n_slots, n_stream,
                                                       lambda m, n: m * N_COLS + n)
    hbm = pl.BlockSpec(memory_space=pl.ANY)
    kv_shape = jax.ShapeDtypeStruct((depth, rows, N_HEADS, HEAD_DIM), F32)
    stacked = [k_all, v_all] if aliased else []
    first_stacked = 5 + n_stream
    return pl.pallas_call(
        functools.partial(_inproj_prompt_kernel, n_stream=n_stream, layer=layer, aliased=aliased),
        grid_spec=pltpu.PrefetchScalarGridSpec(
            num_scalar_prefetch=1,
            grid=(rows // tm, N_COLS),
            in_specs=in_specs + page_specs + [hbm] * len(stacked),
            out_specs=[act_spec, kmean_spec, hbm, hbm],
            scratch_shapes=[
                pltpu.VMEM((tm, COL), F32),
                pltpu.VMEM((tm, COL), F32),
                pltpu.SemaphoreType.DMA((2, depth, N_HEADS)),
            ],
        ),
        out_shape=[jax.ShapeDtypeStruct((rows, N_IN), BF16), kmean_shape, kv_shape, kv_shape],
        input_output_aliases={first_stacked: 2, first_stacked + 1: 3} if aliased else {},
        compiler_params=_params("arbitrary", "arbitrary"),
        name="in_proj_prompt",
    )(page_table.reshape(-1), h, w_all, ln_g.reshape(-1, 1, COL), ln_b.reshape(-1, 1, COL),
      *([cache_k] * n_stream), *stacked)


SEL_W = 8
MASKED = -1e30


def _slope_parts():
    slopes = (2.0 ** (-8.0 * np.arange(1, N_HEADS + 1, dtype=np.float32) / N_HEADS)).astype(np.float32)
    parts, rest = [], slopes
    while rest.any():
        part = rest.astype(BF16).astype(np.float32)
        parts.append(part)
        rest = rest - part
    return slopes, np.stack(parts)


def _feature_tables(seq):
    _, parts = _slope_parts()
    n_parts = parts.shape[0]
    width = SEL_W + 4 * n_parts
    q_rows = -(-width // 16) * 16 - SEL_W
    assert seq // MOBA_BLOCK <= SEL_W and width <= HEAD_DIM
    pos = np.arange(seq)
    blk_of = pos // MOBA_BLOCK
    a = (blk_of * MOBA_BLOCK).astype(np.float32)
    r = (pos % MOBA_BLOCK).astype(np.float32)
    kfeat = np.zeros((N_HEADS, seq, HEAD_DIM), np.float32)
    qfeat = np.zeros((N_HEADS, q_rows, seq), np.float32)
    for m in range(SEL_W):
        kfeat[:, :, m] = blk_of == m
    for h in range(N_HEADS):
        for p in range(n_parts):
            c = SEL_W + 4 * p
            kfeat[h, :, c:c + 2] = -parts[p, h]
            kfeat[h, :, c + 2], kfeat[h, :, c + 3] = a, r
            qfeat[h, 4 * p], qfeat[h, 4 * p + 1] = a, r
            qfeat[h, 4 * p + 2:4 * p + 4] = parts[p, h]
    return jnp.asarray(kfeat.astype(BF16)), jnp.asarray(qfeat)


def _moba_prompt_kernel(*refs, n_stream):
    q_ref, k_ref, v_ref, za_ref, kfeat_ref, qfeat_ref = refs[1:7]
    pages = refs[7:7 + n_stream]
    o_ref, kmean_ref, qa_ref, ka_ref, va_ref, s_ref = refs[7 + n_stream:]
    blk = MOBA_BLOCK
    seq = k_ref.shape[0]
    nblk = seq // blk
    hd = HEAD_DIM
    _stream_block_means(pages, kmean_ref, pl.program_id(0) * pl.num_programs(1) + pl.program_id(1))

    @pl.when((pl.program_id(0) == 0) & (pl.program_id(1) == 0))
    def _():
        lane = lax.broadcasted_iota(jnp.int32, (seq, hd), 1)
        va_ref[:, hd:] = jnp.where(lane == 0, 1.0, 0.0).astype(BF16)

    q = q_ref[...]
    k = k_ref[...]
    qa_ref[:, :hd] = q
    ka_ref[:, :hd] = k
    ka_ref[:, hd:] = kfeat_ref[...]
    va_ref[:, :hd] = v_ref[...]

    km = jnp.sum(k.astype(F32).reshape(nblk, blk, hd), axis=1) * (1.0 / blk)
    if nblk < SEL_W:
        km = jnp.concatenate([km, jnp.zeros((SEL_W - nblk, hd), F32)], axis=0)
    km_hi = km.astype(BF16).astype(F32)
    gt = _dot_nt(jnp.concatenate([km_hi, km - km_hi], axis=0).astype(BF16), q)
    gate = gt[:SEL_W] + gt[SEL_W:]

    blk_row = lax.broadcasted_iota(jnp.int32, gate.shape, 0)
    blk_own = lax.broadcasted_iota(jnp.int32, gate.shape, 1) // blk
    eligible = blk_row < blk_own
    ge = jnp.where(eligible, gate, NEG_INF)
    rank = jnp.zeros(gate.shape, jnp.int32)
    for m in range(nblk):
        other = ge[m:m + 1, :]
        ahead = (other > ge) | ((other == ge) & (m < blk_row))
        rank = rank + ahead.astype(jnp.int32)
    allowed = (eligible & (rank < TOPK)) | (blk_row == blk_own)
    feat_t = jnp.concatenate([jnp.where(allowed, 0.0, MASKED), qfeat_ref[...]], axis=0).astype(BF16)

    eye = (lax.broadcasted_iota(jnp.int32, (blk, blk), 0) == lax.broadcasted_iota(jnp.int32, (blk, blk), 1))
    eye = jnp.where(eye, 1.0, 0.0).astype(BF16)
    pad = jnp.zeros((hd - feat_t.shape[0], blk), BF16)
    for t in range(nblk):
        rows = slice(t * blk, (t + 1) * blk)
        qa_ref[rows, hd:] = _dot_nt(eye, jnp.concatenate([feat_t[:, rows], pad], axis=0)).astype(BF16)

    causal = lax.broadcasted_iota(jnp.int32, (blk, blk), 0) >= lax.broadcasted_iota(jnp.int32, (blk, blk), 1)
    for t in range(nblk):
        rows = slice(t * blk, (t + 1) * blk)
        qt = qa_ref[rows, :]
        first = t * (t + 1) // 2
        for n in range(t + 1):
            s = _dot_nt(qt, ka_ref[n * blk:(n + 1) * blk, :])
            if n == t:
                s = jnp.where(causal, s, MASKED)
            s_ref[first + n] = s
        m_el = s_ref[first]
        for n in range(1, t + 1):
            m_el = jnp.maximum(m_el, s_ref[first + n])
        m = jnp.max(m_el, axis=1, keepdims=True)
        acc = None
        for n in range(t + 1):
            p = jnp.exp(s_ref[first + n] - m).astype(BF16)
            pv = _dot(p, va_ref[n * blk:(n + 1) * blk, :])
            acc = pv if acc is None else acc + pv
        out = acc[:, :hd] / acc[:, hd:hd + 1] * za_ref[rows, :].astype(F32)
        o_ref[rows, :] = out.astype(o_ref.dtype)


def _moba_prompt(act, batch, seq, cache_k, page_table, layer, first_slot, n_slots, n_stream):
    assert batch * N_HEADS * n_stream >= n_slots
    kfeat, qfeat = _feature_tables(seq)
    nblk = seq // MOBA_BLOCK
    n_pairs = nblk * (nblk + 1) // 2
    per = COL // HEAD_DIM
    head = lambda c0: pl.BlockSpec((seq, HEAD_DIM), lambda b, h, *_: (b, c0 * per + h))
    page_specs, kmean_spec, kmean_shape = _stream_plan(cache_k, page_table, layer, first_slot, n_slots, n_stream,
                                                       lambda b, h: b * N_HEADS + h)
    return pl.pallas_call(
        functools.partial(_moba_prompt_kernel, n_stream=n_stream),
        grid_spec=pltpu.PrefetchScalarGridSpec(
            num_scalar_prefetch=1,
            grid=(batch, N_HEADS),
            in_specs=[
                head(C_Q),
                head(C_K),
                head(C_V),
                head(C_ZA),
                pl.BlockSpec((None, seq, HEAD_DIM), lambda b, h, *_: (h, 0, 0)),
                pl.BlockSpec((None, qfeat.shape[1], seq), lambda b, h, *_: (h, 0, 0)),
            ] + page_specs,
            out_specs=[pl.BlockSpec((seq, HEAD_DIM), lambda b, h, *_: (b, h)), kmean_spec],
            scratch_shapes=[
                pltpu.VMEM((seq, 2 * HEAD_DIM), BF16),
                pltpu.VMEM((seq, 2 * HEAD_DIM), BF16),
                pltpu.VMEM((seq, 2 * HEAD_DIM), BF16),
                pltpu.VMEM((n_pairs, MOBA_BLOCK, MOBA_BLOCK), F32),
            ],
        ),
        out_shape=[jax.ShapeDtypeStruct((batch * seq, D_ATT), BF16), kmean_shape],
        compiler_params=_params("arbitrary", "arbitrary"),
        name="moba_prompt",
    )(page_table.reshape(-1), act, act, act, act, kfeat, qfeat, *([cache_k] * n_stream))


def _sgu_prompt_kernel(ws_ref, bs_ref, vn_ref, u_ref, zb_ref, o_ref):
    row = lax.broadcasted_iota(jnp.int32, (CHUNK, CHUNK), 0)
    col = lax.broadcasted_iota(jnp.int32, (CHUNK, CHUNK), 1)
    for g in range(N_GROUPS):
        ws = jnp.where(row >= col, ws_ref[g], 0.0).astype(BF16)
        bs = bs_ref[g]
        cols = slice(g * GROUP_DIM, (g + 1) * GROUP_DIM)
        for c in range(vn_ref.shape[0] // CHUNK):
            r = slice(c * CHUNK, (c + 1) * CHUNK)
            mix = _dot(ws, vn_ref[r, cols]) + bs
            o_ref[r, cols] = (u_ref[r, cols].astype(F32) * mix * zb_ref[r, cols].astype(F32)).astype(o_ref.dtype)


def _sgu_prompt(act, w_s, b_s, layer, tm):
    rows = act.shape[0]
    col = lambda c: pl.BlockSpec((tm, COL), lambda m: (m, c))
    return pl.pallas_call(
        _sgu_prompt_kernel,
        grid=(rows // tm,),
        in_specs=[
            pl.BlockSpec((None, N_GROUPS, CHUNK, CHUNK), lambda m: (layer, 0, 0, 0)),
            pl.BlockSpec((None, N_GROUPS, CHUNK, 1), lambda m: (layer, 0, 0, 0)),
            col(C_VB),
            col(C_U),
            col(C_ZB),
        ],
        out_specs=pl.BlockSpec((tm, D_SGU), lambda m: (m, 0)),
        out_shape=jax.ShapeDtypeStruct((rows, D_SGU), BF16),
        compiler_params=_params("arbitrary"),
        name="sgu_prompt",
    )(w_s, b_s.reshape(-1, N_GROUPS, CHUNK, 1), act, act, act)


def _outproj_kernel(a_ref, b_ref, ga0_ref, ga1_ref, gb0_ref, gb1_ref, x_ref, gate_ref, wpa_ref, wpb_ref, wout_ref,
                    g_ref, *rest, last):
    ya = _dot(a_ref[...], wpa_ref[...])
    yb = _dot(b_ref[...], wpb_ref[...])
    sga = jnp.concatenate([ga0_ref[...], ga1_ref[...]], axis=1).astype(F32)
    sgb = jnp.concatenate([gb0_ref[...], gb1_ref[...]], axis=1).astype(F32)
    mix = sga * ya + sgb * yb
    x = x_ref[...] + gate_ref[...] * _dot(mix.astype(BF16), wout_ref[...])
    xn = x * lax.rsqrt(jnp.mean(x * x, axis=-1, keepdims=True) + EPS) * g_ref[...]
    if last:
        (y_ref,) = rest
        y_ref[...] = xn
    else:
        scale_ref, shift_ref, x_out_ref, h_ref = rest
        x_out_ref[...] = x
        h_ref[...] = (xn * (1.0 + scale_ref[...]) + shift_ref[...]).astype(h_ref.dtype)


def _outproj(a, b, act, x, gate, w_pa, w_pb, w_out, layer, tm, rows_per_batch, norm_g, scale=None, shift=None):
    rows, d = x.shape
    last = scale is None
    gate, gate_spec = _row_vec_spec(gate, tm, rows_per_batch)
    resident = lambda w: pl.BlockSpec((None,) + w.shape[1:], lambda m: (layer, 0, 0), pipeline_mode=pl.Buffered(1))
    col = lambda c: pl.BlockSpec((tm, COL), lambda m: (m, c))
    tile = pl.BlockSpec((tm, d), lambda m: (m, 0))
    in_specs = [
        pl.BlockSpec((tm, D_ATT), lambda m: (m, 0)),
        pl.BlockSpec((tm, D_SGU), lambda m: (m, 0)),
        col(C_GA),
        col(C_GA + 1),
        col(C_GB),
        col(C_GB + 1),
        tile,
        gate_spec,
        resident(w_pa),
        resident(w_pb),
        resident(w_out),
        pl.BlockSpec((None, 1, d), lambda m: (0, 0, 0)),
    ]
    args = [a, b, act, act, act, act, x, gate, w_pa, w_pb, w_out, norm_g]
    if last:
        out_specs, out_shape = tile, jax.ShapeDtypeStruct((rows, d), F32)
    else:
        scale, scale_spec = _row_vec_spec(scale, tm, rows_per_batch)
        shift, shift_spec = _row_vec_spec(shift, tm, rows_per_batch)
        in_specs += [scale_spec, shift_spec]
        args += [scale, shift]
        out_specs = [tile, tile]
        out_shape = [jax.ShapeDtypeStruct((rows, d), F32), jax.ShapeDtypeStruct((rows, d), BF16)]
    return pl.pallas_call(
        functools.partial(_outproj_kernel, last=last),
        grid=(rows // tm,),
        in_specs=in_specs,
        out_specs=out_specs,
        out_shape=out_shape,
        compiler_params=_params("arbitrary"),
        name="out_proj",
    )(*args)


def _mix_sample_kernel(a_ref, b_ref, ga_ref, gb_ref, wpa_ref, wpb_ref, o_ref):
    o_ref[...] = ga_ref[...] * _dot(a_ref[...], wpa_ref[...]) + gb_ref[...] * _dot(b_ref[...], wpb_ref[...])


def _residual_sample_kernel(mix_ref, x_ref, gate_ref, wout_ref, o_ref):
    o_ref[...] = x_ref[...] + gate_ref[...] * _dot(mix_ref[...], wout_ref[...])


def _outproj_sample(a, b, act, x, gate, w_pa, w_pb, w_out, layer):
    rows, d = x.shape
    tn = 512
    per = COL // tn
    full = lambda arr: pl.BlockSpec(arr.shape, lambda n: (0, 0))
    cols = lambda c0: pl.BlockSpec((rows, tn), lambda n: (0, c0 * per + n))
    weight = lambda w: pl.BlockSpec((None, w.shape[1], tn), lambda n: (layer, 0, n))
    mix = pl.pallas_call(
        _mix_sample_kernel,
        grid=(d // tn,),
        in_specs=[full(a), full(b), cols(C_GA), cols(C_GB), weight(w_pa), weight(w_pb)],
        out_specs=cols(0),
        out_shape=jax.ShapeDtypeStruct((rows, d), F32),
        compiler_params=_params("arbitrary"),
        name="mix_sample",
    )(a, b, act, act, w_pa, w_pb)
    return pl.pallas_call(
        _residual_sample_kernel,
        grid=(d // tn,),
        in_specs=[full(mix), cols(0), cols(0), weight(w_out)],
        out_specs=cols(0),
        out_shape=jax.ShapeDtypeStruct((rows, d), F32),
        compiler_params=_params("arbitrary"),
        name="residual_sample",
    )(mix, x, gate, w_out)


def _select_kernel(q_ref, km_ref, sel_ref):
    q = q_ref[...].astype(F32)
    km = km_ref[...]
    g = jnp.sum(km * q[None], axis=-1, keepdims=True)
    nb = g.shape[0]
    idx = lax.broadcasted_iota(jnp.int32, g.shape, 0)
    for r in range(TOPK):
        best = jnp.max(g, axis=0, keepdims=True)
        pick = jnp.min(jnp.where(g == best, idx, nb), axis=0, keepdims=True)
        sel_ref[r:r + 1] = pick
        g = jnp.where(idx == pick, NEG_INF, g)


def _select(q, kmean):
    b, nh, hd = q.shape
    nb = kmean.shape[1]
    sel = pl.pallas_call(
        _select_kernel,
        grid=(b,),
        in_specs=[
            pl.BlockSpec((None, nh, hd), lambda i: (i, 0, 0)),
            pl.BlockSpec((None, nb, nh, hd), lambda i: (i, 0, 0, 0)),
        ],
        out_specs=pl.BlockSpec((None, TOPK, nh, 1), lambda i: (i, 0, 0, 0)),
        out_shape=jax.ShapeDtypeStruct((b, TOPK, nh, 1), jnp.int32),
        compiler_params=_params("arbitrary"),
        name="moba_select",
    )(q, kmean)
    return sel.reshape(b, TOPK, nh)


def _moba_sample_kernel(sel_ref, pt_ref, slopes_ref, q_ref, kn_ref, vn_ref, za_ref, ck_ref, cv_ref, o_ref,
                        kbuf, vbuf, sem, *, layer, past, page, n_pages):
    nh = q_ref.shape[0]
    ppb = MOBA_BLOCK // page
    ntile = TOPK * ppb
    b = pl.program_id(0)
    slot = b % 2

    def block_of(seq, h, t):
        return sel_ref[(seq * TOPK + t // ppb) * nh + h]

    def copies(seq):
        out = []
        for h in range(nh):
            for t in range(ntile):
                pg = pt_ref[seq * n_pages + block_of(seq, h, t) * ppb + t % ppb]
                dst = (seq % 2, h, t)
                out.append(pltpu.make_async_copy(ck_ref.at[layer, pg, :, h, :], kbuf.at[dst], sem.at[(0,) + dst]))
                out.append(pltpu.make_async_copy(cv_ref.at[layer, pg, :, h, :], vbuf.at[dst], sem.at[(1,) + dst]))
        return out

    @pl.when(b == 0)
    def _():
        for cp in copies(b):
            cp.start()

    @pl.when(b + 1 < pl.num_programs(0))
    def _():
        for cp in copies(b + 1):
            cp.start()

    for cp in copies(b):
        cp.wait()

    pos = lax.broadcasted_iota(jnp.int32, (page, 1), 0)
    for h in range(nh):
        slope = slopes_ref[h]
        q = q_ref[h]
        s_self = jnp.sum(kn_ref[h] * q, axis=1, keepdims=True)
        scores = []
        m = s_self
        for t in range(ntile):
            dist = (past - (block_of(b, h, t) * MOBA_BLOCK + (t % ppb) * page) - pos).astype(F32)
            s = jnp.sum(kbuf[slot, h, t] * q, axis=1, keepdims=True) - slope * dist
            scores.append(s)
            m = jnp.maximum(m, jnp.max(s, axis=0, keepdims=True))
        p_self = jnp.exp(s_self - m)
        denom = p_self
        out = p_self * vn_ref[h]
        for t in range(ntile):
            p = jnp.exp(scores[t] - m)
            denom = denom + jnp.sum(p, axis=0, keepdims=True)
            out = out + jnp.sum(p * vbuf[slot, h, t], axis=0, keepdims=True)
        o_ref[h] = out / denom * za_ref[h]


def _moba_sample(sel, page_table, slopes, q, k_new, v_new, za, cache_k, cache_v, layer):
    b, nh, _, hd = q.shape
    page = cache_k.shape[2]
    n_pages = page_table.shape[1]
    past = n_pages * page
    assert past % MOBA_BLOCK == 0 and past // MOBA_BLOCK >= TOPK
    ntile = TOPK * (MOBA_BLOCK // page)
    vec = pl.BlockSpec((None, nh, 1, hd), lambda bi, s, p: (bi, 0, 0, 0))
    hbm = pl.BlockSpec(memory_space=pl.ANY)
    return pl.pallas_call(
        functools.partial(_moba_sample_kernel, layer=layer, past=past, page=page, n_pages=n_pages),
        grid_spec=pltpu.PrefetchScalarGridSpec(
            num_scalar_prefetch=2,
            grid=(b,),
            in_specs=[pl.BlockSpec(memory_space=pltpu.SMEM), vec, vec, vec, vec, hbm, hbm],
            out_specs=vec,
            scratch_shapes=[
                pltpu.VMEM((2, nh, ntile, page, hd), F32),
                pltpu.VMEM((2, nh, ntile, page, hd), F32),
                pltpu.SemaphoreType.DMA((2, 2, nh, ntile)),
            ],
        ),
        out_shape=jax.ShapeDtypeStruct((b, nh, 1, hd), F32),
        compiler_params=_params("arbitrary"),
        name="moba_sample",
    )(sel.reshape(-1), page_table.reshape(-1), slopes, q, k_new, v_new, za, cache_k, cache_v)


def _sgu_sample_kernel(u_ref, vn_ref, zb_ref, w0_ref, b0_ref, o_ref):
    mix = w0_ref[...] * vn_ref[...] + b0_ref[...]
    o_ref[...] = (u_ref[...].astype(F32) * mix * zb_ref[...].astype(F32)).astype(o_ref.dtype)


def _sgu_sample(act, vn, w0, b0):
    rows = act.shape[0]
    blk = lambda c: pl.BlockSpec((rows, COL), lambda i: (0, c))
    vec = pl.BlockSpec((1, COL), lambda i: (0, 0))
    return pl.pallas_call(
        _sgu_sample_kernel,
        grid=(1,),
        in_specs=[blk(C_U), blk(0), blk(C_ZB), vec, vec],
        out_specs=blk(0),
        out_shape=jax.ShapeDtypeStruct((rows, D_SGU), act.dtype),
        compiler_params=_params("arbitrary"),
        name="sgu_sample",
    )(act, vn, act, w0, b0)


def kernel(x_prompt, x_sample, cache_k, cache_v, page_table, c_prompt, c_sample, norm_g, w_ada, b_ada, w_in,
           sgu_ln_g, sgu_ln_b, w_s, b_s, w_proj_a, w_proj_b, w_out, final_g):
    batch, seq, d = x_prompt.shape
    dec_batch, dec_seq, _ = x_sample.shape
    assert dec_seq == 1
    depth = w_in.shape[0]
    slopes = jnp.asarray(_slope_parts()[0])

    mod = _mod(jnp.concatenate([c_prompt, c_sample], axis=0), w_ada, b_ada)
    shift, scale, gate = mod[:, :, :d], mod[:, :, d:2 * d], mod[:, :, 2 * d:]

    xp = x_prompt.reshape(batch * seq, d)
    xs = x_sample.reshape(dec_batch, d)
    tm_norm, tm_in, tm_sgu, tm_out, tm_s = 512, 1024, 1024, 256, dec_batch
    w_in_b, w_pa, w_pb, w_o = (w.astype(BF16) for w in (w_in, w_proj_a, w_proj_b, w_out))
    hp = _norm_mod(xp, norm_g, 0, scale[0, :batch], shift[0, :batch], tm_norm, seq, BF16)

    n_pages = page_table.size
    ppb = MOBA_BLOCK // cache_k.shape[2]
    in_slots = min(n_pages, (batch * seq // tm_in) * N_COLS * IN_STREAM) // ppb * ppb
    n_blocks = page_table.shape[1] // ppb

    kp, vp = None, None
    ks_l, vs_l, us_l = [], [], []
    for l in range(depth):
        act, kmean_a, kp, vp = _inproj_prompt(hp, w_in_b, sgu_ln_g, sgu_ln_b, l, tm_in, cache_k, page_table,
                                              in_slots, IN_STREAM, kp, vp)
        att, kmean_b = _moba_prompt(act, batch, seq, cache_k, page_table, l, in_slots, n_pages - in_slots,
                                    MOBA_STREAM)
        kmean = jnp.concatenate([kmean_a, kmean_b], axis=0).reshape(dec_batch, n_blocks, N_HEADS, HEAD_DIM)
        sgu = _sgu_prompt(act, w_s, b_s, l, tm_sgu)
        if l + 1 < depth:
            xp, hp = _outproj(att, sgu, act, xp, gate[l, :batch], w_pa, w_pb, w_o, l, tm_out, seq,
                              norm_g[l + 1].reshape(1, 1, d), scale[l + 1, :batch], shift[l + 1, :batch])
        else:
            y_prompt = _outproj(att, sgu, act, xp, gate[l, :batch], w_pa, w_pb, w_o, l, tm_out, seq,
                                final_g.reshape(1, 1, d))

        hs = _norm_mod(xs, norm_g, l, scale[l, batch:], shift[l, batch:], tm_s, 1, F32)
        act_s, side_s = _inproj(hs, w_in, sgu_ln_g, sgu_ln_b, l, tm_s)
        ks, vs, us = side_s[:, :COL], side_s[:, COL:2 * COL], side_s[:, 2 * COL:]
        heads = lambda a: a.reshape(dec_batch, N_HEADS, 1, HEAD_DIM)
        q_s = act_s[:, C_Q * COL:(C_Q + 1) * COL]
        za_s = act_s[:, C_ZA * COL:(C_ZA + 1) * COL]
        sel = _select(q_s.reshape(dec_batch, N_HEADS, HEAD_DIM), kmean)
        att_s = _moba_sample(sel, page_table, slopes, heads(q_s), heads(ks), heads(vs), heads(za_s),
                             cache_k, cache_v, l)
        w0 = jnp.repeat(w_s[l, :, 0, 0], GROUP_DIM).reshape(1, D_SGU)
        b0 = jnp.repeat(b_s[l, :, 0], GROUP_DIM).reshape(1, D_SGU)
        sgu_s = _sgu_sample(act_s, us, w0, b0)
        xs = _outproj_sample(att_s.reshape(dec_batch, D_ATT), sgu_s, act_s, xs, gate[l, batch:],
                             w_proj_a, w_proj_b, w_out, l)
        ks_l.append(ks.reshape(dec_batch, 1, N_HEADS, HEAD_DIM))
        vs_l.append(vs.reshape(dec_batch, 1, N_HEADS, HEAD_DIM))
        us_l.append(us.reshape(dec_batch, 1, D_SGU))

    y_sample = _final_norm(xs, final_g, dec_batch).reshape(dec_batch, 1, d)
    prompt_kv = lambda a: a.reshape(depth, batch, seq, N_HEADS, HEAD_DIM)
    return (y_prompt.reshape(batch, seq, d), y_sample, prompt_kv(kp), prompt_kv(vp), jnp.stack(ks_l),
            jnp.stack(vs_l), jnp.stack(us_l))
```

```python
import functools

import numpy as np
import jax
import jax.numpy as jnp
from jax import lax
from jax.experimental import pallas as pl
from jax.experimental.pallas import tpu as pltpu

D_MODEL = 2048
HEAD_DIM = 128
D_ATT = D_MODEL // 2
N_HEADS = D_ATT // HEAD_DIM
GROUP_DIM = 128
D_SGU = D_MODEL // 2
N_GROUPS = D_SGU // GROUP_DIM
MOBA_BLOCK = 256
TOPK = 3
CHUNK = 128
EPS = 1e-6
SCALE = HEAD_DIM ** -0.5
N_IN = 4 * D_ATT + 3 * D_SGU + 2 * D_MODEL

COL = 1024
N_COLS = N_IN // COL
C_Q, C_K, C_V, C_ZA, C_U, C_VB, C_ZB, C_GA, C_GB = 0, 1, 2, 3, 4, 5, 6, 7, 9

VMEM_LIMIT = 56 * 1024 * 1024
F32 = jnp.float32
BF16 = jnp.bfloat16
NEG_INF = float("-inf")


def _params(*sem):
    return pltpu.CompilerParams(dimension_semantics=sem, vmem_limit_bytes=VMEM_LIMIT)


def _dot(a, b):
    precision = lax.Precision.HIGHEST if a.dtype == F32 else None
    return jnp.dot(a, b, preferred_element_type=F32, precision=precision)


def _gelu(x):
    return 0.5 * x * (1.0 + lax.erf(x * (2.0 ** -0.5)))


def _dot_nt(a, b):
    return lax.dot_general(a, b, (((1,), (1,)), ((), ())), preferred_element_type=F32)


def _mod_kernel(c_ref, w_ref, b_ref, o_ref):
    c = c_ref[...]
    o_ref[...] = _dot(c * jax.nn.sigmoid(c), w_ref[...]) + b_ref[...]


def _mod(c_all, w_ada, b_ada):
    depth, d, n3 = w_ada.shape
    rows = c_all.shape[0]
    tn = 768
    return pl.pallas_call(
        _mod_kernel,
        grid=(depth, n3 // tn),
        in_specs=[
            pl.BlockSpec((rows, d), lambda l, n: (0, 0)),
            pl.BlockSpec((None, d, tn), lambda l, n: (l, 0, n)),
            pl.BlockSpec((None, 1, tn), lambda l, n: (l, 0, n)),
        ],
        out_specs=pl.BlockSpec((None, rows, tn), lambda l, n: (l, 0, n)),
        out_shape=jax.ShapeDtypeStruct((depth, rows, n3), F32),
        compiler_params=_params("arbitrary", "arbitrary"),
        name="adaln_mod",
    )(c_all, w_ada, b_ada.reshape(depth, 1, n3))


def _norm_kernel(x_ref, g_ref, scale_ref, shift_ref, o_ref):
    x = x_ref[...]
    y = x * lax.rsqrt(jnp.mean(x * x, axis=-1, keepdims=True) + EPS) * g_ref[...]
    o_ref[...] = (y * (1.0 + scale_ref[...]) + shift_ref[...]).astype(o_ref.dtype)


def _final_norm_kernel(x_ref, g_ref, o_ref):
    x = x_ref[...]
    o_ref[...] = x * lax.rsqrt(jnp.mean(x * x, axis=-1, keepdims=True) + EPS) * g_ref[...]


def _row_vec_spec(vec, tm, rows_per_batch):
    d = vec.shape[-1]
    if rows_per_batch == 1:
        return vec, pl.BlockSpec((tm, d), lambda m: (m, 0))
    assert rows_per_batch % tm == 0
    per = rows_per_batch // tm
    return vec.reshape(vec.shape[0], 1, d), pl.BlockSpec((None, 1, d), lambda m: (m // per, 0, 0))


def _norm_mod(x, g_all, layer, scale, shift, tm, rows_per_batch, out_dtype):
    rows, d = x.shape
    scale, scale_spec = _row_vec_spec(scale, tm, rows_per_batch)
    shift, shift_spec = _row_vec_spec(shift, tm, rows_per_batch)
    return pl.pallas_call(
        _norm_kernel,
        grid=(rows // tm,),
        in_specs=[
            pl.BlockSpec((tm, d), lambda m: (m, 0)),
            pl.BlockSpec((None, 1, d), lambda m: (layer, 0, 0)),
            scale_spec,
            shift_spec,
        ],
        out_specs=pl.BlockSpec((tm, d), lambda m: (m, 0)),
        out_shape=jax.ShapeDtypeStruct((rows, d), out_dtype),
        compiler_params=_params("arbitrary"),
        name="norm_mod",
    )(x, g_all.reshape(-1, 1, d), scale, shift)


def _final_norm(x, g, tm):
    rows, d = x.shape
    return pl.pallas_call(
        _final_norm_kernel,
        grid=(rows // tm,),
        in_specs=[pl.BlockSpec((tm, d), lambda m: (m, 0)), pl.BlockSpec((1, d), lambda m: (0, 0))],
        out_specs=pl.BlockSpec((tm, d), lambda m: (m, 0)),
        out_shape=jax.ShapeDtypeStruct((rows, d), F32),
        compiler_params=_params("arbitrary"),
        name="final_norm",
    )(x, g.reshape(1, d))


INPROJ_CHUNK = 256
IN_STREAM = 10
MOBA_STREAM = 6


def _stream_plan(cache_k, page_table, layer, first_slot, n_slots, n_stream, step_of):
    _, _, page, nh, hd = cache_k.shape
    ppb = MOBA_BLOCK // page
    assert n_stream % ppb == 0 and first_slot % ppb == 0 and n_slots % ppb == 0

    def page_spec(j):
        def index(*ids_and_table):
            *ids, table = ids_and_table
            rel = jnp.minimum(step_of(*ids) * n_stream + j, n_slots - ppb + j % ppb)
            return (layer, table[first_slot + rel], 0, 0, 0)
        return pl.BlockSpec((None, None, page, nh, hd), index)

    in_specs = [page_spec(j) for j in range(n_stream)]
    out_spec = pl.BlockSpec((n_slots // ppb, nh, hd), lambda *_: (0, 0, 0))
    return in_specs, out_spec, jax.ShapeDtypeStruct((n_slots // ppb, nh, hd), F32)


def _stream_block_means(pages, kmean_ref, step):
    ppb = MOBA_BLOCK // pages[0].shape[0]
    per_step = len(pages) // ppb
    ways = 8

    def page_sum(ref):
        p = ref[...]
        return p.reshape(ways, p.shape[0] // ways, *p.shape[1:]).sum(axis=1).sum(axis=0)

    for i in range(per_step):
        tot = page_sum(pages[i * ppb])
        for j in range(1, ppb):
            tot = tot + page_sum(pages[i * ppb + j])
        kmean_ref[jnp.minimum(step * per_step + i, kmean_ref.shape[0] - 1)] = tot * (1.0 / MOBA_BLOCK)


def _inproj_columns(n, h_ref, w_ref, lng_ref, lnb_ref, act_ref, keep_k, keep_v, keep_vn, rider=None):
    tm = h_ref.shape[0]
    chunk = min(INPROJ_CHUNK, tm)

    def column_block(epilogue):
        if rider is not None:
            rider()
        for c in range(tm // chunk):
            rows = pl.ds(c * chunk, chunk)
            acc = _dot(h_ref[rows, :], w_ref[...])
            act_ref[rows, :] = epilogue(rows, acc).astype(act_ref.dtype)

    def kept(keep):
        def epilogue(rows, acc):
            keep(rows, acc)
            return acc
        return epilogue

    def norm_gelu(rows, acc):
        a = _gelu(acc)
        mu = jnp.mean(a, axis=-1, keepdims=True)
        c = a - mu
        var = jnp.mean(c * c, axis=-1, keepdims=True)
        vn = c * lax.rsqrt(var + EPS) * lng_ref[...] + lnb_ref[...]
        if keep_vn is not None:
            keep_vn(rows, vn)
        return vn

    @pl.when(n >= C_GA)
    def _():
        column_block(lambda rows, acc: jax.nn.sigmoid(acc))

    @pl.when(n == C_Q)
    def _():
        column_block(lambda rows, acc: acc * SCALE)

    @pl.when(n == C_K)
    def _():
        column_block(kept(keep_k))

    @pl.when(n == C_V)
    def _():
        column_block(kept(keep_v))

    @pl.when((n == C_ZA) | (n == C_ZB))
    def _():
        column_block(lambda rows, acc: jax.nn.silu(acc))

    @pl.when(n == C_U)
    def _():
        column_block(lambda rows, acc: _gelu(acc))

    @pl.when(n == C_VB)
    def _():
        column_block(norm_gelu)


def _inproj_kernel(h_ref, w_ref, lng_ref, lnb_ref, act_ref, side_ref):
    def keep(rows, value):
        side_ref[rows, :] = value

    _inproj_columns(pl.program_id(1), h_ref, w_ref, lng_ref, lnb_ref, act_ref, keep, keep, keep)


def _inproj_prompt_kernel(*refs, n_stream, layer, aliased):
    table_ref, h_ref, w_ref, lng_ref, lnb_ref = refs[:5]
    pages = refs[5:5 + n_stream]
    act_ref, kmean_ref, k_out, v_out, kbuf, vbuf, sem = refs[5 + n_stream + (2 if aliased else 0):]
    m, n = pl.program_id(0), pl.program_id(1)
    tm = h_ref.shape[0]
    last_step = (m == pl.num_programs(0) - 1) & (n == pl.num_programs(1) - 1)
    slots = (layer,) if aliased else tuple(range(layer, k_out.shape[0]))

    def head_copies(buf, out, which, tile):
        return [pltpu.make_async_copy(buf.at[:, pl.ds(h * HEAD_DIM, HEAD_DIM)],
                                      out.at[slot, pl.ds(tile * tm, tm), h, :], sem.at[which, i, h])
                for i, slot in enumerate(slots) for h in range(N_HEADS)]

    def before(buf, out, which):
        @pl.when(m > 0)
        def _():
            for cp in head_copies(buf, out, which, m - 1):
                cp.wait()

    @pl.when(n == C_K)
    def _():
        before(kbuf, k_out, 0)

    @pl.when(n == C_V)
    def _():
        before(vbuf, v_out, 1)

    def keep_k(rows, value):
        kbuf[rows, :] = value

    def keep_v(rows, value):
        vbuf[rows, :] = value

    _inproj_columns(n, h_ref, w_ref, lng_ref, lnb_ref, act_ref, keep_k, keep_v, None,
                    rider=lambda: _stream_block_means(pages, kmean_ref, m * pl.num_programs(1) + n))

    @pl.when(n == C_K)
    def _():
        for cp in head_copies(kbuf, k_out, 0, m):
            cp.start()

    @pl.when(n == C_V)
    def _():
        for cp in head_copies(vbuf, v_out, 1, m):
            cp.start()

    @pl.when(last_step)
    def _():
        for cp in head_copies(kbuf, k_out, 0, m) + head_copies(vbuf, v_out, 1, m):
            cp.wait()


def _inproj_specs(d, layer, tm):
    vec = pl.BlockSpec((None, 1, COL), lambda m, n, *_: (layer, 0, 0))
    in_specs = [
        pl.BlockSpec((tm, d), lambda m, n, *_: (m, 0)),
        pl.BlockSpec((None, d, COL), lambda m, n, *_: (layer, 0, n)),
        vec,
        vec,
    ]
    return in_specs, pl.BlockSpec((tm, COL), lambda m, n, *_: (m, n))


def _inproj(h, w_all, ln_g, ln_b, layer, tm):
    rows, d = h.shape
    assert h.dtype == w_all.dtype

    def side_index(m, n):
        return (m, jnp.where(n >= C_V, 1, 0) + jnp.where(n >= C_VB, 1, 0))

    in_specs, act_spec = _inproj_specs(d, layer, tm)
    return pl.pallas_call(
        _inproj_kernel,
        grid=(rows // tm, N_COLS),
        in_specs=in_specs,
        out_specs=[act_spec, pl.BlockSpec((tm, COL), side_index)],
        out_shape=[jax.ShapeDtypeStruct((rows, N_IN), h.dtype), jax.ShapeDtypeStruct((rows, 3 * COL), F32)],
        compiler_params=_params("arbitrary", "arbitrary"),
        name="in_proj",
    )(h, w_all, ln_g.reshape(-1, 1, COL), ln_b.reshape(-1, 1, COL))


def _inproj_prompt(h, w_all, ln_g, ln_b, layer, tm, cache_k, page_table, n_slots, n_stream, k_all, v_all):
    rows, d = h.shape
    depth = w_all.shape[0]
    aliased = k_all is not None
    assert aliased == (layer > 0) and (rows // tm) * N_COLS * n_stream >= n_slots
    in_specs, act_spec = _inproj_specs(d, layer, tm)
    page_specs, kmean_spec, kmean_shape = _stream_plan(cache_k, page_table, layer, 0, n_slots, n_stream,
                                                       lambda m, n: m * N_COLS + n)
    hbm = pl.BlockSpec(memory_space=pl.ANY)
    kv_shape = jax.ShapeDtypeStruct((depth, rows, N_HEADS, HEAD_DIM), F32)
    stacked = [k_all, v_all] if aliased else []
    first_stacked = 5 + n_stream
    return pl.pallas_call(
        functools.partial(_inproj_prompt_kernel, n_stream=n_stream, layer=layer, aliased=aliased),
        grid_spec=pltpu.PrefetchScalarGridSpec(
            num_scalar_prefetch=1,
            grid=(rows // tm, N_COLS),
            in_specs=in_specs + page_specs + [hbm] * len(stacked),
            out_specs=[act_spec, kmean_spec, hbm, hbm],
            scratch_shapes=[
                pltpu.VMEM((tm, COL), F32),
                pltpu.VMEM((tm, COL), F32),
                pltpu.SemaphoreType.DMA((2, depth, N_HEADS)),
            ],
        ),
        out_shape=[jax.ShapeDtypeStruct((rows, N_IN), BF16), kmean_shape, kv_shape, kv_shape],
        input_output_aliases={first_stacked: 2, first_stacked + 1: 3} if aliased else {},
        compiler_params=_params("arbitrary", "arbitrary"),
        name="in_proj_prompt",
    )(page_table.reshape(-1), h, w_all, ln_g.reshape(-1, 1, COL), ln_b.reshape(-1, 1, COL),
      *([cache_k] * n_stream), *stacked)


SEL_W = 8
MASKED = -1e30


def _slope_parts():
    slopes = (2.0 ** (-8.0 * np.arange(1, N_HEADS + 1, dtype=np.float32) / N_HEADS)).astype(np.float32)
    parts, rest = [], slopes
    while rest.any():
        part = rest.astype(BF16).astype(np.float32)
        parts.append(part)
        rest = rest - part
    return slopes, np.stack(parts)


def _feature_tables(seq):
    _, parts = _slope_parts()
    n_parts = parts.shape[0]
    width = SEL_W + 4 * n_parts
    q_rows = -(-width // 16) * 16 - SEL_W
    assert seq // MOBA_BLOCK <= SEL_W and width <= HEAD_DIM
    pos = np.arange(seq)
    blk_of = pos // MOBA_BLOCK
    a = (blk_of * MOBA_BLOCK).astype(np.float32)
    r = (pos % MOBA_BLOCK).astype(np.float32)
    kfeat = np.zeros((N_HEADS, seq, HEAD_DIM), np.float32)
    qfeat = np.zeros((N_HEADS, q_rows, seq), np.float32)
    for m in range(SEL_W):
        kfeat[:, :, m] = blk_of == m
    for h in range(N_HEADS):
        for p in range(n_parts):
            c = SEL_W + 4 * p
            kfeat[h, :, c:c + 2] = -parts[p, h]
            kfeat[h, :, c + 2], kfeat[h, :, c + 3] = a, r
            qfeat[h, 4 * p], qfeat[h, 4 * p + 1] = a, r
            qfeat[h, 4 * p + 2:4 * p + 4] = parts[p, h]
    return jnp.asarray(kfeat.astype(BF16)), jnp.asarray(qfeat)


def _moba_prompt_kernel(*refs, n_stream):
    q_ref, k_ref, v_ref, za_ref, kfeat_ref, qfeat_ref = refs[1:7]
    pages = refs[7:7 + n_stream]
    o_ref, kmean_ref, qa_ref, ka_ref, va_ref, s_ref = refs[7 + n_stream:]
    blk = MOBA_BLOCK
    seq = k_ref.shape[0]
    nblk = seq // blk
    hd = HEAD_DIM
    _stream_block_means(pages, kmean_ref, pl.program_id(0) * pl.num_programs(1) + pl.program_id(1))

    @pl.when((pl.program_id(0) == 0) & (pl.program_id(1) == 0))
    def _():
        lane = lax.broadcasted_iota(jnp.int32, (seq, hd), 1)
        va_ref[:, hd:] = jnp.where(lane == 0, 1.0, 0.0).astype(BF16)

    q = q_ref[...]
    k = k_ref[...]
    qa_ref[:, :hd] = q
    ka_ref[:, :hd] = k
    ka_ref[:, hd:] = kfeat_ref[...]
    va_ref[:, :hd] = v_ref[...]

    km = jnp.sum(k.astype(F32).reshape(nblk, blk, hd), axis=1) * (1.0 / blk)
    if nblk < SEL_W:
        km = jnp.concatenate([km, jnp.zeros((SEL_W - nblk, hd), F32)], axis=0)
    km_hi = km.astype(BF16).astype(F32)
    gt = _dot_nt(jnp.concatenate([km_hi, km - km_hi], axis=0).astype(BF16), q)
    gate = gt[:SEL_W] + gt[SEL_W:]

    blk_row = lax.broadcasted_iota(jnp.int32, gate.shape, 0)
    blk_own = lax.broadcasted_iota(jnp.int32, gate.shape, 1) // blk
    eligible = blk_row < blk_own
    ge = jnp.where(eligible, gate, NEG_INF)
    rank = jnp.zeros(gate.shape, jnp.int32)
    for m in range(nblk):
        other = ge[m:m + 1, :]
        ahead = (other > ge) | ((other == ge) & (m < blk_row))
        rank = rank + ahead.astype(jnp.int32)
    allowed = (eligible & (rank < TOPK)) | (blk_row == blk_own)
    feat_t = jnp.concatenate([jnp.where(allowed, 0.0, MASKED), qfeat_ref[...]], axis=0).astype(BF16)

    eye = (lax.broadcasted_iota(jnp.int32, (blk, blk), 0) == lax.broadcasted_iota(jnp.int32, (blk, blk), 1))
    eye = jnp.where(eye, 1.0, 0.0).astype(BF16)
    pad = jnp.zeros((hd - feat_t.shape[0], blk), BF16)
    for t in range(nblk):
        rows = slice(t * blk, (t + 1) * blk)
        qa_ref[rows, hd:] = _dot_nt(eye, jnp.concatenate([feat_t[:, rows], pad], axis=0)).astype(BF16)

    causal = lax.broadcasted_iota(jnp.int32, (blk, blk), 0) >= lax.broadcasted_iota(jnp.int32, (blk, blk), 1)

    def first_slot(t):
        return t * (t + 1) // 2

    def scores(t):
        qt = qa_ref[t * blk:(t + 1) * blk, :]
        for n in range(t + 1):
            s = _dot_nt(qt, ka_ref[n * blk:(n + 1) * blk, :])
            if n == t:
                s = jnp.where(causal, s, MASKED)
            s_ref[first_slot(t) + n] = s

    def attend(t):
        rows = slice(t * blk, (t + 1) * blk)
        first = first_slot(t)
        m_el = s_ref[first]
        for n in range(1, t + 1):
            m_el = jnp.maximum(m_el, s_ref[first + n])
        m = jnp.max(m_el, axis=1, keepdims=True)
        acc = None
        for n in range(t + 1):
            p = jnp.exp(s_ref[first + n] - m).astype(BF16)
            pv = _dot(p, va_ref[n * blk:(n + 1) * blk, :])
            acc = pv if acc is None else acc + pv
        out = acc[:, :hd] / acc[:, hd:hd + 1] * za_ref[rows, :].astype(F32)
        o_ref[rows, :] = out.astype(o_ref.dtype)

    ahead = 1
    for t in range(min(ahead, nblk)):
        scores(t)
    for t in range(nblk):
        if t + ahead < nblk:
            scores(t + ahead)
        attend(t)


def _moba_prompt(act, batch, seq, cache_k, page_table, layer, first_slot, n_slots, n_stream):
    assert batch * N_HEADS * n_stream >= n_slots
    kfeat, qfeat = _feature_tables(seq)
    nblk = seq // MOBA_BLOCK
    n_pairs = nblk * (nblk + 1) // 2
    per = COL // HEAD_DIM
    head = lambda c0: pl.BlockSpec((seq, HEAD_DIM), lambda b, h, *_: (b, c0 * per + h))
    page_specs, kmean_spec, kmean_shape = _stream_plan(cache_k, page_table, layer, first_slot, n_slots, n_stream,
                                                       lambda b, h: b * N_HEADS + h)
    return pl.pallas_call(
        functools.partial(_moba_prompt_kernel, n_stream=n_stream),
        grid_spec=pltpu.PrefetchScalarGridSpec(
            num_scalar_prefetch=1,
            grid=(batch, N_HEADS),
            in_specs=[
                head(C_Q),
                head(C_K),
                head(C_V),
                head(C_ZA),
                pl.BlockSpec((None, seq, HEAD_DIM), lambda b, h, *_: (h, 0, 0)),
                pl.BlockSpec((None, qfeat.shape[1], seq), lambda b, h, *_: (h, 0, 0)),
            ] + page_specs,
            out_specs=[pl.BlockSpec((seq, HEAD_DIM), lambda b, h, *_: (b, h)), kmean_spec],
            scratch_shapes=[
                pltpu.VMEM((seq, 2 * HEAD_DIM), BF16),
                pltpu.VMEM((seq, 2 * HEAD_DIM), BF16),
                pltpu.VMEM((seq, 2 * HEAD_DIM), BF16),
                pltpu.VMEM((n_pairs, MOBA_BLOCK, MOBA_BLOCK), F32),
            ],
        ),
        out_shape=[jax.ShapeDtypeStruct((batch * seq, D_ATT), BF16), kmean_shape],
        compiler_params=_params("arbitrary", "arbitrary"),
        name="moba_prompt",
    )(page_table.reshape(-1), act, act, act, act, kfeat, qfeat, *([cache_k] * n_stream))


def _sgu_prompt_kernel(ws_ref, bs_ref, vn_ref, u_ref, zb_ref, o_ref):
    row = lax.broadcasted_iota(jnp.int32, (CHUNK, CHUNK), 0)
    col = lax.broadcasted_iota(jnp.int32, (CHUNK, CHUNK), 1)
    for g in range(N_GROUPS):
        ws = jnp.where(row >= col, ws_ref[g], 0.0).astype(BF16)
        bs = bs_ref[g]
        cols = slice(g * GROUP_DIM, (g + 1) * GROUP_DIM)
        for c in range(vn_ref.shape[0] // CHUNK):
            r = slice(c * CHUNK, (c + 1) * CHUNK)
            mix = _dot(ws, vn_ref[r, cols]) + bs
            o_ref[r, cols] = (u_ref[r, cols].astype(F32) * mix * zb_ref[r, cols].astype(F32)).astype(o_ref.dtype)


def _sgu_prompt(act, w_s, b_s, layer, tm):
    rows = act.shape[0]
    col = lambda c: pl.BlockSpec((tm, COL), lambda m: (m, c))
    return pl.pallas_call(
        _sgu_prompt_kernel,
        grid=(rows // tm,),
        in_specs=[
            pl.BlockSpec((None, N_GROUPS, CHUNK, CHUNK), lambda m: (layer, 0, 0, 0)),
            pl.BlockSpec((None, N_GROUPS, CHUNK, 1), lambda m: (layer, 0, 0, 0)),
            col(C_VB),
            col(C_U),
            col(C_ZB),
        ],
        out_specs=pl.BlockSpec((tm, D_SGU), lambda m: (m, 0)),
        out_shape=jax.ShapeDtypeStruct((rows, D_SGU), BF16),
        compiler_params=_params("arbitrary"),
        name="sgu_prompt",
    )(w_s, b_s.reshape(-1, N_GROUPS, CHUNK, 1), act, act, act)


def _outproj_kernel(a_ref, b_ref, ga0_ref, ga1_ref, gb0_ref, gb1_ref, x_ref, gate_ref, wpa_ref, wpb_ref, wout_ref,
                    g_ref, *rest, last):
    ya = _dot(a_ref[...], wpa_ref[...])
    yb = _dot(b_ref[...], wpb_ref[...])
    sga = jnp.concatenate([ga0_ref[...], ga1_ref[...]], axis=1).astype(F32)
    sgb = jnp.concatenate([gb0_ref[...], gb1_ref[...]], axis=1).astype(F32)
    mix = sga * ya + sgb * yb
    x = x_ref[...] + gate_ref[...] * _dot(mix.astype(BF16), wout_ref[...])
    xn = x * lax.rsqrt(jnp.mean(x * x, axis=-1, keepdims=True) + EPS) * g_ref[...]
    if last:
        (y_ref,) = rest
        y_ref[...] = xn
    else:
        scale_ref, shift_ref, x_out_ref, h_ref = rest
        x_out_ref[...] = x
        h_ref[...] = (xn * (1.0 + scale_ref[...]) + shift_ref[...]).astype(h_ref.dtype)


def _outproj(a, b, act, x, gate, w_pa, w_pb, w_out, layer, tm, rows_per_batch, norm_g, scale=None, shift=None):
    rows, d = x.shape
    last = scale is None
    gate, gate_spec = _row_vec_spec(gate, tm, rows_per_batch)
    resident = lambda w: pl.BlockSpec((None,) + w.shape[1:], lambda m: (layer, 0, 0), pipeline_mode=pl.Buffered(1))
    col = lambda c: pl.BlockSpec((tm, COL), lambda m: (m, c))
    tile = pl.BlockSpec((tm, d), lambda m: (m, 0))
    in_specs = [
        pl.BlockSpec((tm, D_ATT), lambda m: (m, 0)),
        pl.BlockSpec((tm, D_SGU), lambda m: (m, 0)),
        col(C_GA),
        col(C_GA + 1),
        col(C_GB),
        col(C_GB + 1),
        tile,
        gate_spec,
        resident(w_pa),
        resident(w_pb),
        resident(w_out),
        pl.BlockSpec((None, 1, d), lambda m: (0, 0, 0)),
    ]
    args = [a, b, act, act, act, act, x, gate, w_pa, w_pb, w_out, norm_g]
    if last:
        out_specs, out_shape = tile, jax.ShapeDtypeStruct((rows, d), F32)
    else:
        scale, scale_spec = _row_vec_spec(scale, tm, rows_per_batch)
        shift, shift_spec = _row_vec_spec(shift, tm, rows_per_batch)
        in_specs += [scale_spec, shift_spec]
        args += [scale, shift]
        out_specs = [tile, tile]
        out_shape = [jax.ShapeDtypeStruct((rows, d), F32), jax.ShapeDtypeStruct((rows, d), BF16)]
    return pl.pallas_call(
        functools.partial(_outproj_kernel, last=last),
        grid=(rows // tm,),
        in_specs=in_specs,
        out_specs=out_specs,
        out_shape=out_shape,
        compiler_params=_params("arbitrary"),
        name="out_proj",
    )(*args)


def _mix_sample_kernel(a_ref, b_ref, ga_ref, gb_ref, wpa_ref, wpb_ref, o_ref):
    o_ref[...] = ga_ref[...] * _dot(a_ref[...], wpa_ref[...]) + gb_ref[...] * _dot(b_ref[...], wpb_ref[...])


def _residual_sample_kernel(mix_ref, x_ref, gate_ref, wout_ref, o_ref):
    o_ref[...] = x_ref[...] + gate_ref[...] * _dot(mix_ref[...], wout_ref[...])


def _outproj_sample(a, b, act, x, gate, w_pa, w_pb, w_out, layer):
    rows, d = x.shape
    tn = 512
    per = COL // tn
    full = lambda arr: pl.BlockSpec(arr.shape, lambda n: (0, 0))
    cols = lambda c0: pl.BlockSpec((rows, tn), lambda n: (0, c0 * per + n))
    weight = lambda w: pl.BlockSpec((None, w.shape[1], tn), lambda n: (layer, 0, n))
    mix = pl.pallas_call(
        _mix_sample_kernel,
        grid=(d // tn,),
        in_specs=[full(a), full(b), cols(C_GA), cols(C_GB), weight(w_pa), weight(w_pb)],
        out_specs=cols(0),
        out_shape=jax.ShapeDtypeStruct((rows, d), F32),
        compiler_params=_params("arbitrary"),
        name="mix_sample",
    )(a, b, act, act, w_pa, w_pb)
    return pl.pallas_call(
        _residual_sample_kernel,
        grid=(d // tn,),
        in_specs=[full(mix), cols(0), cols(0), weight(w_out)],
        out_specs=cols(0),
        out_shape=jax.ShapeDtypeStruct((rows, d), F32),
        compiler_params=_params("arbitrary"),
        name="residual_sample",
    )(mix, x, gate, w_out)


def _select_kernel(q_ref, km_ref, sel_ref):
    q = q_ref[...].astype(F32)
    km = km_ref[...]
    g = jnp.sum(km * q[None], axis=-1, keepdims=True)
    nb = g.shape[0]
    idx = lax.broadcasted_iota(jnp.int32, g.shape, 0)
    for r in range(TOPK):
        best = jnp.max(g, axis=0, keepdims=True)
        pick = jnp.min(jnp.where(g == best, idx, nb), axis=0, keepdims=True)
        sel_ref[r:r + 1] = pick
        g = jnp.where(idx == pick, NEG_INF, g)


def _select(q, kmean):
    b, nh, hd = q.shape
    nb = kmean.shape[1]
    sel = pl.pallas_call(
        _select_kernel,
        grid=(b,),
        in_specs=[
            pl.BlockSpec((None, nh, hd), lambda i: (i, 0, 0)),
            pl.BlockSpec((None, nb, nh, hd), lambda i: (i, 0, 0, 0)),
        ],
        out_specs=pl.BlockSpec((None, TOPK, nh, 1), lambda i: (i, 0, 0, 0)),
        out_shape=jax.ShapeDtypeStruct((b, TOPK, nh, 1), jnp.int32),
        compiler_params=_params("arbitrary"),
        name="moba_select",
    )(q, kmean)
    return sel.reshape(b, TOPK, nh)


def _moba_sample_kernel(sel_ref, pt_ref, slopes_ref, q_ref, kn_ref, vn_ref, za_ref, ck_ref, cv_ref, o_ref,
                        kbuf, vbuf, sem, *, layer, past, page, n_pages):
    nh = q_ref.shape[0]
    ppb = MOBA_BLOCK // page
    ntile = TOPK * ppb
    b = pl.program_id(0)
    slot = b % 2

    def block_of(seq, h, t):
        return sel_ref[(seq * TOPK + t // ppb) * nh + h]

    def copies(seq):
        out = []
        for h in range(nh):
            for t in range(ntile):
                pg = pt_ref[seq * n_pages + block_of(seq, h, t) * ppb + t % ppb]
                dst = (seq % 2, h, t)
                out.append(pltpu.make_async_copy(ck_ref.at[layer, pg, :, h, :], kbuf.at[dst], sem.at[(0,) + dst]))
                out.append(pltpu.make_async_copy(cv_ref.at[layer, pg, :, h, :], vbuf.at[dst], sem.at[(1,) + dst]))
        return out

    @pl.when(b == 0)
    def _():
        for cp in copies(b):
            cp.start()

    @pl.when(b + 1 < pl.num_programs(0))
    def _():
        for cp in copies(b + 1):
            cp.start()

    for cp in copies(b):
        cp.wait()

    pos = lax.broadcasted_iota(jnp.int32, (page, 1), 0)
    for h in range(nh):
        slope = slopes_ref[h]
        q = q_ref[h]
        s_self = jnp.sum(kn_ref[h] * q, axis=1, keepdims=True)
        scores = []
        m = s_self
        for t in range(ntile):
            dist = (past - (block_of(b, h, t) * MOBA_BLOCK + (t % ppb) * page) - pos).astype(F32)
            s = jnp.sum(kbuf[slot, h, t] * q, axis=1, keepdims=True) - slope * dist
            scores.append(s)
            m = jnp.maximum(m, jnp.max(s, axis=0, keepdims=True))
        p_self = jnp.exp(s_self - m)
        denom = p_self
        out = p_self * vn_ref[h]
        for t in range(ntile):
            p = jnp.exp(scores[t] - m)
            denom = denom + jnp.sum(p, axis=0, keepdims=True)
            out = out + jnp.sum(p * vbuf[slot, h, t], axis=0, keepdims=True)
        o_ref[h] = out / denom * za_ref[h]


def _moba_sample(sel, page_table, slopes, q, k_new, v_new, za, cache_k, cache_v, layer):
    b, nh, _, hd = q.shape
    page = cache_k.shape[2]
    n_pages = page_table.shape[1]
    past = n_pages * page
    assert past % MOBA_BLOCK == 0 and past // MOBA_BLOCK >= TOPK
    ntile = TOPK * (MOBA_BLOCK // page)
    vec = pl.BlockSpec((None, nh, 1, hd), lambda bi, s, p: (bi, 0, 0, 0))
    hbm = pl.BlockSpec(memory_space=pl.ANY)
    return pl.pallas_call(
        functools.partial(_moba_sample_kernel, layer=layer, past=past, page=page, n_pages=n_pages),
        grid_spec=pltpu.PrefetchScalarGridSpec(
            num_scalar_prefetch=2,
            grid=(b,),
            in_specs=[pl.BlockSpec(memory_space=pltpu.SMEM), vec, vec, vec, vec, hbm, hbm],
            out_specs=vec,
            scratch_shapes=[
                pltpu.VMEM((2, nh, ntile, page, hd), F32),
                pltpu.VMEM((2, nh, ntile, page, hd), F32),
                pltpu.SemaphoreType.DMA((2, 2, nh, ntile)),
            ],
        ),
        out_shape=jax.ShapeDtypeStruct((b, nh, 1, hd), F32),
        compiler_params=_params("arbitrary"),
        name="moba_sample",
    )(sel.reshape(-1), page_table.reshape(-1), slopes, q, k_new, v_new, za, cache_k, cache_v)


def _sgu_sample_kernel(u_ref, vn_ref, zb_ref, w0_ref, b0_ref, o_ref):
    mix = w0_ref[...] * vn_ref[...] + b0_ref[...]
    o_ref[...] = (u_ref[...].astype(F32) * mix * zb_ref[...].astype(F32)).astype(o_ref.dtype)


def _sgu_sample(act, vn, w0, b0):
    rows = act.shape[0]
    blk = lambda c: pl.BlockSpec((rows, COL), lambda i: (0, c))
    vec = pl.BlockSpec((1, COL), lambda i: (0, 0))
    return pl.pallas_call(
        _sgu_sample_kernel,
        grid=(1,),
        in_specs=[blk(C_U), blk(0), blk(C_ZB), vec, vec],
        out_specs=blk(0),
        out_shape=jax.ShapeDtypeStruct((rows, D_SGU), act.dtype),
        compiler_params=_params("arbitrary"),
        name="sgu_sample",
    )(act, vn, act, w0, b0)


def kernel(x_prompt, x_sample, cache_k, cache_v, page_table, c_prompt, c_sample, norm_g, w_ada, b_ada, w_in,
           sgu_ln_g, sgu_ln_b, w_s, b_s, w_proj_a, w_proj_b, w_out, final_g):
    batch, seq, d = x_prompt.shape
    dec_batch, dec_seq, _ = x_sample.shape
    assert dec_seq == 1
    depth = w_in.shape[0]
    slopes = jnp.asarray(_slope_parts()[0])

    mod = _mod(jnp.concatenate([c_prompt, c_sample], axis=0), w_ada, b_ada)
    shift, scale, gate = mod[:, :, :d], mod[:, :, d:2 * d], mod[:, :, 2 * d:]

    xp = x_prompt.reshape(batch * seq, d)
    xs = x_sample.reshape(dec_batch, d)
    tm_norm, tm_in, tm_sgu, tm_out, tm_s = 512, 1024, 1024, 256, dec_batch
    w_in_b, w_pa, w_pb, w_o = (w.astype(BF16) for w in (w_in, w_proj_a, w_proj_b, w_out))
    hp = _norm_mod(xp, norm_g, 0, scale[0, :batch], shift[0, :batch], tm_norm, seq, BF16)

    n_pages = page_table.size
    ppb = MOBA_BLOCK // cache_k.shape[2]
    in_slots = min(n_pages, (batch * seq // tm_in) * N_COLS * IN_STREAM) // ppb * ppb
    n_blocks = page_table.shape[1] // ppb

    kp, vp = None, None
    ks_l, vs_l, us_l = [], [], []
    for l in range(depth):
        act, kmean_a, kp, vp = _inproj_prompt(hp, w_in_b, sgu_ln_g, sgu_ln_b, l, tm_in, cache_k, page_table,
                                              in_slots, IN_STREAM, kp, vp)
        att, kmean_b = _moba_prompt(act, batch, seq, cache_k, page_table, l, in_slots, n_pages - in_slots,
                                    MOBA_STREAM)
        kmean = jnp.concatenate([kmean_a, kmean_b], axis=0).reshape(dec_batch, n_blocks, N_HEADS, HEAD_DIM)
        sgu = _sgu_prompt(act, w_s, b_s, l, tm_sgu)
        if l + 1 < depth:
            xp, hp = _outproj(att, sgu, act, xp, gate[l, :batch], w_pa, w_pb, w_o, l, tm_out, seq,
                              norm_g[l + 1].reshape(1, 1, d), scale[l + 1, :batch], shift[l + 1, :batch])
        else:
            y_prompt = _outproj(att, sgu, act, xp, gate[l, :batch], w_pa, w_pb, w_o, l, tm_out, seq,
                                final_g.reshape(1, 1, d))

        hs = _norm_mod(xs, norm_g, l, scale[l, batch:], shift[l, batch:], tm_s, 1, F32)
        act_s, side_s = _inproj(hs, w_in, sgu_ln_g, sgu_ln_b, l, tm_s)
        ks, vs, us = side_s[:, :COL], side_s[:, COL:2 * COL], side_s[:, 2 * COL:]
        heads = lambda a: a.reshape(dec_batch, N_HEADS, 1, HEAD_DIM)
        q_s = act_s[:, C_Q * COL:(C_Q + 1) * COL]
        za_s = act_s[:, C_ZA * COL:(C_ZA + 1) * COL]
        sel = _select(q_s.reshape(dec_batch, N_HEADS, HEAD_DIM), kmean)
        att_s = _moba_sample(sel, page_table, slopes, heads(q_s), heads(ks), heads(vs), heads(za_s),
                             cache_k, cache_v, l)
        w0 = jnp.repeat(w_s[l, :, 0, 0], GROUP_DIM).reshape(1, D_SGU)
        b0 = jnp.repeat(b_s[l, :, 0], GROUP_DIM).reshape(1, D_SGU)
        sgu_s = _sgu_sample(act_s, us, w0, b0)
        xs = _outproj_sample(att_s.reshape(dec_batch, D_ATT), sgu_s, act_s, xs, gate[l, batch:],
                             w_proj_a, w_proj_b, w_out, l)
        ks_l.append(ks.reshape(dec_batch, 1, N_HEADS, HEAD_DIM))
        vs_l.append(vs.reshape(dec_batch, 1, N_HEADS, HEAD_DIM))
        us_l.append(us.reshape(dec_batch, 1, D_SGU))

    y_sample = _final_norm(xs, final_g, dec_batch).reshape(dec_batch, 1, d)
    prompt_kv = lambda a: a.reshape(depth, batch, seq, N_HEADS, HEAD_DIM)
    return (y_prompt.reshape(batch, seq, d), y_sample, prompt_kv(kp), prompt_kv(vp), jnp.stack(ks_l),
            jnp.stack(vs_l), jnp.stack(us_l))
```
